```python
import jax, jax.numpy as jnp
from jax import lax
import numpy as np

D_MODEL = 2048
BATCH = 4
SEQ = 4096
DEPTH = 2

D_MIX = D_MODEL
ATTN_WIDTH = D_MIX // 2
HGRN_WIDTH = D_MIX - ATTN_WIDTH
HEAD_DIM = 64
N_Q_HEADS = ATTN_WIDTH // HEAD_DIM
N_KV_HEADS = 4
Q_PER_KV = N_Q_HEADS // N_KV_HEADS
WINDOW = 128
ATTN_BLOCK = WINDOW
ROPE_THETA = 10000.0
MASK_VALUE = -1e30
HGRN_EXPAND = 128
HGRN_HEADS = HGRN_WIDTH // HGRN_EXPAND
HGRN_VDIM = HGRN_WIDTH // HGRN_HEADS
HGRN_CHUNK = 64
D_FF = ((8 * D_MODEL // 3 + 255) // 256) * 256
D_PLE = 256
RMS_EPS = 1e-6
KV_WIDTH = N_KV_HEADS * HEAD_DIM
IN_SIZES = (ATTN_WIDTH, KV_WIDTH, KV_WIDTH, HGRN_WIDTH, HGRN_WIDTH, HGRN_WIDTH, HGRN_WIDTH)
SPLIT_POINTS = tuple(int(v) for v in np.cumsum(IN_SIZES)[:-1])
D_IN = sum(IN_SIZES)

kernel_name = "hymba_swa_sink_hgrn2_sandwich_ple"


def rms_norm(x, gain):
    xf = x.astype(jnp.float32)
    y = xf * lax.rsqrt(jnp.mean(xf * xf, axis=-1, keepdims=True) + RMS_EPS)
    return (y * gain.astype(jnp.float32)).astype(x.dtype)


def rope(x, positions):
    half = HEAD_DIM // 2
    inv_freq = ROPE_THETA ** (-jnp.arange(half, dtype=jnp.float32) / half)
    ang = positions.astype(jnp.float32)[..., None] * inv_freq
    cos = jnp.cos(ang)[:, :, None, :]
    sin = jnp.sin(ang)[:, :, None, :]
    xf = x.astype(jnp.float32)
    x1, x2 = xf[..., :half], xf[..., half:]
    out = jnp.concatenate([x1 * cos - x2 * sin, x2 * cos + x1 * sin], axis=-1)
    return out.astype(x.dtype)


def sliding_window_attention(q, k, v, sinks):
    B, S = q.shape[0], q.shape[1]
    nb = S // ATTN_BLOCK
    L = ATTN_BLOCK
    qb = q.reshape(B, nb, L, N_KV_HEADS, Q_PER_KV, HEAD_DIM)
    kb = k.reshape(B, nb, L, N_KV_HEADS, HEAD_DIM)
    vb = v.reshape(B, nb, L, N_KV_HEADS, HEAD_DIM)
    prev = lambda t: jnp.concatenate([jnp.zeros_like(t[:, :1]), t[:, :-1]], axis=1)
    kk = jnp.concatenate([prev(kb), kb], axis=2)
    vv = jnp.concatenate([prev(vb), vb], axis=2)
    scores = jnp.einsum('bnqhgd,bnkhd->bnhgqk', qb, kk,
                        preferred_element_type=jnp.float32) * (HEAD_DIM ** -0.5)
    qi = jnp.arange(L)[:, None] + L
    ki = jnp.arange(2 * L)[None, :]
    rel = qi - ki
    band = (rel >= 0) & (rel < WINDOW)
    valid = band[None] & ((jnp.arange(nb)[:, None, None] > 0) | (ki >= L)[None])
    scores = jnp.where(valid[None, :, None, None], scores, MASK_VALUE)
    sink = sinks.astype(jnp.float32).reshape(N_KV_HEADS, Q_PER_KV)[None, None, :, :, None, None]
    m = jnp.maximum(jnp.max(scores, axis=-1, keepdims=True), sink)
    e = jnp.exp(scores - m)
    probs = e / (jnp.sum(e, axis=-1, keepdims=True) + jnp.exp(sink - m))
    out = jnp.einsum('bnhgqk,bnkhd->bnqhgd', probs.astype(vv.dtype), vv)
    return out.reshape(B, S, N_Q_HEADS * HEAD_DIM)


def hgrn2_chunkwise(q, k, v, log_f):
    B, S = q.shape[0], q.shape[1]
    nc = S // HGRN_CHUNK
    C = HGRN_CHUNK

    def to_chunks(t):
        return t.astype(jnp.float32).reshape(B, nc, C, HGRN_HEADS, t.shape[-1]).transpose(1, 0, 3, 2, 4)

    qc, kc, vc, gc = to_chunks(q), to_chunks(k), to_chunks(v), to_chunks(log_f)
    causal = jnp.tril(jnp.ones((C, C), dtype=bool))

    def step(state, inp):
        q_, k_, v_, g_ = inp
        b = jnp.cumsum(g_, axis=2)
        diff = b[:, :, :, None, :] - b[:, :, None, :, :]
        decay = jnp.exp(jnp.where(causal[None, None, :, :, None], diff, MASK_VALUE))
        a = jnp.einsum('bhtk,bhsk,bhtsk->bhts', q_, k_, decay)
        o = jnp.einsum('bhts,bhsv->bhtv', a, v_) + \
            jnp.einsum('bhtk,bhkv->bhtv', q_ * jnp.exp(b), state)
        b_last = b[:, :, -1:, :]
        new_state = state * jnp.exp(b_last)[:, :, 0, :, None] + \
            jnp.einsum('bhsk,bhsv->bhkv', k_ * jnp.exp(b_last - b), v_)
        return new_state, o

    s0 = jnp.zeros((B, HGRN_HEADS, HGRN_EXPAND, HGRN_VDIM), jnp.float32)
    _, o = lax.scan(step, s0, (qc, kc, vc, gc))
    return o.transpose(1, 0, 3, 2, 4).reshape(B, S, HGRN_HEADS, HGRN_VDIM)


def hybrid_mixer(h, positions, w_in, sinks, lb, attn_gain, hgrn_gain, w_out):
    B, S = h.shape[0], h.shape[1]
    proj = h @ w_in
    q, k, v, hq, hf, hi, hg = jnp.split(proj, SPLIT_POINTS, axis=-1)
    q = rope(q.reshape(B, S, N_Q_HEADS, HEAD_DIM), positions)
    k = rope(k.reshape(B, S, N_KV_HEADS, HEAD_DIM), positions)
    v = v.reshape(B, S, N_KV_HEADS, HEAD_DIM)
    attn = rms_norm(sliding_window_attention(q, k, v, sinks), attn_gain)
    z = hf.astype(jnp.float32)
    lbf = lb.astype(jnp.float32)
    f = lbf + (1.0 - lbf) * jax.nn.sigmoid(z)
    log_f = jnp.log(f)
    k_in = (1.0 - lbf) * jax.nn.sigmoid(-z)
    hq_ = jax.nn.silu(hq.astype(jnp.float32))
    shp = (B, S, HGRN_HEADS, HGRN_EXPAND)
    o = hgrn2_chunkwise(hq_.reshape(shp), k_in.reshape(shp),
                        hi.reshape(B, S, HGRN_HEADS, HGRN_VDIM), log_f.reshape(shp))
    o = rms_norm(o, hgrn_gain.reshape(HGRN_HEADS, HGRN_VDIM)).reshape(B, S, HGRN_WIDTH)
    hgrn = (o * jax.nn.silu(hg.astype(jnp.float32))).astype(h.dtype)
    return jnp.concatenate([attn, hgrn], axis=-1) @ w_out


def setup_inputs(seed: int = 0) -> dict:
    key = jax.random.key(seed)
    ks = jax.random.split(key, 24)
    f32 = jnp.float32
    nrm = lambda k, shape, scale: jax.random.normal(k, shape, f32) * scale
    gain = lambda k, shape: 1.0 + 0.05 * jax.random.normal(k, shape, f32)
    offsets = jax.random.randint(ks[2], (BATCH, 1), 0, 1024, dtype=jnp.int32)
    positions = offsets + jnp.arange(SEQ, dtype=jnp.int32)[None, :]
    return {
        "x": nrm(ks[0], (BATCH, SEQ, D_MODEL), 1.0),
        "p": nrm(ks[1], (DEPTH, BATCH, SEQ, D_PLE), 1.0),
        "positions": positions,
        "w_in": nrm(ks[3], (DEPTH, D_MODEL, D_IN), D_MODEL ** -0.5),
        "attn_sinks": nrm(ks[4], (DEPTH, N_Q_HEADS), 1.0),
        "hgrn_lb_logits": nrm(ks[5], (DEPTH, HGRN_WIDTH), 0.5),
        "attn_out_gain": gain(ks[6], (DEPTH, ATTN_WIDTH)),
        "hgrn_out_gain": gain(ks[7], (DEPTH, HGRN_WIDTH)),
        "w_out": nrm(ks[8], (DEPTH, D_MIX, D_MODEL), D_MIX ** -0.5),
        "pre_mix_gain": gain(ks[9], (DEPTH, D_MODEL)),
        "post_mix_gain": gain(ks[10], (DEPTH, D_MODEL)),
        "pre_ffn_gain": gain(ks[11], (DEPTH, D_MODEL)),
        "post_ffn_gain": gain(ks[12], (DEPTH, D_MODEL)),
        "w_ffn_gate": nrm(ks[13], (DEPTH, D_MODEL, D_FF), D_MODEL ** -0.5),
        "w_ffn_up": nrm(ks[14], (DEPTH, D_MODEL, D_FF), D_MODEL ** -0.5),
        "w_ffn_down": nrm(ks[15], (DEPTH, D_FF, D_MODEL), D_FF ** -0.5),
        "ple_gain": gain(ks[16], (DEPTH, D_MODEL)),
        "w_ple_gate": nrm(ks[17], (DEPTH, D_MODEL, D_MODEL), D_MODEL ** -0.5),
        "w_ple_proj": nrm(ks[18], (DEPTH, D_PLE, D_MODEL), 0.5 * D_PLE ** -0.5),
    }


def reference(x, p, positions, w_in, attn_sinks, hgrn_lb_logits, attn_out_gain, hgrn_out_gain,
              w_out, pre_mix_gain, post_mix_gain, pre_ffn_gain, post_ffn_gain,
              w_ffn_gate, w_ffn_up, w_ffn_down, ple_gain, w_ple_gate, w_ple_proj):
    lb_soft = jax.nn.softmax(hgrn_lb_logits.astype(jnp.float32), axis=0)
    lower_bounds = jnp.cumsum(lb_soft, axis=0) - lb_soft[0:1]
    for i in range(DEPTH):
        h = rms_norm(x, pre_mix_gain[i])
        m = hybrid_mixer(h, positions, w_in[i], attn_sinks[i], lower_bounds[i],
                         attn_out_gain[i], hgrn_out_gain[i], w_out[i])
        x = x + rms_norm(m, post_mix_gain[i])
        h = rms_norm(x, pre_ffn_gain[i])
        f = (jax.nn.silu(h @ w_ffn_gate[i]) * (h @ w_ffn_up[i])) @ w_ffn_down[i]
        x = x + rms_norm(f, post_ffn_gain[i])
        gate = jax.nn.sigmoid(rms_norm(x, ple_gain[i]) @ w_ple_gate[i])
        x = x + (p[i] @ w_ple_proj[i]) * gate
    return x
```

```python
import functools

import jax
import jax.numpy as jnp
from jax import lax
from jax.experimental import pallas as pl
from jax.experimental.pallas import tpu as pltpu

F32 = jnp.float32
BF16 = jnp.bfloat16

D_MODEL = 2048
DEPTH = 2
ATTN_WIDTH = 1024
HGRN_WIDTH = 1024
HEAD_DIM = 64
N_Q_HEADS = 16
N_KV_HEADS = 4
WINDOW = 128
ROPE_THETA = 10000.0
MASK_VALUE = -1e30
HGRN_HEADS = 8
HGRN_DIM = 128
D_FF = 5632
D_PLE = 256
RMS_EPS = 1e-6
KV_WIDTH = N_KV_HEADS * HEAD_DIM
D_IN = 5632

LANES = 128
SUBLANES = 8
VMEM_LIMIT = 56 * 1024 * 1024

ROW_TILE = 512
COL_TILE = 512
HGRN_CHUNK = 128


def _params(*sem):
    return pltpu.CompilerParams(dimension_semantics=sem, vmem_limit_bytes=VMEM_LIMIT)


def _rms(x, gain):
    ms = jnp.mean(x * x, axis=-1, keepdims=True)
    return x * lax.rsqrt(ms + RMS_EPS) * gain


def _sigmoid(x):
    return 1.0 / (1.0 + jnp.exp(-x))


def _dot(a, b):
    return jnp.dot(a, b, preferred_element_type=F32)


def _dot_nt(a, b):
    return lax.dot_general(a, b, (((1,), (1,)), ((), ())), preferred_element_type=F32)


def _dot_tn(a, b):
    return lax.dot_general(a, b, (((0,), (0,)), ((), ())), preferred_element_type=F32)


def _norm_kernel(x_ref, g_ref, o_ref):
    o_ref[...] = _rms(x_ref[...], g_ref[...]).astype(o_ref.dtype)


def _norm(x, gain):
    n, d = x.shape
    return pl.pallas_call(
        _norm_kernel,
        out_shape=jax.ShapeDtypeStruct((n, d), BF16),
        grid=(n // ROW_TILE,),
        in_specs=[pl.BlockSpec((ROW_TILE, d), lambda i: (i, 0)),
                  pl.BlockSpec((1, d), lambda i: (0, 0))],
        out_specs=pl.BlockSpec((ROW_TILE, d), lambda i: (i, 0)),
        compiler_params=_params("parallel"),
        name="pre_norm",
    )(x, gain)


def _rope_table_kernel(pos_ref, inv_ref, sgn_ref, cos_ref, sin_ref):
    ang = pos_ref[...].astype(F32) * inv_ref[...]
    cos_ref[...] = jnp.cos(ang)
    sin_ref[...] = jnp.sin(ang) * sgn_ref[...]


def _rope_tables(pos, inv, sgn):
    n = pos.shape[0]
    tm = 1024
    return pl.pallas_call(
        _rope_table_kernel,
        out_shape=(jax.ShapeDtypeStruct((n, LANES), F32),
                   jax.ShapeDtypeStruct((n, LANES), F32)),
        grid=(n // tm,),
        in_specs=[pl.BlockSpec((tm, 1), lambda i: (i, 0)),
                  pl.BlockSpec((1, LANES), lambda i: (0, 0)),
                  pl.BlockSpec((1, LANES), lambda i: (0, 0))],
        out_specs=(pl.BlockSpec((tm, LANES), lambda i: (i, 0)),
                   pl.BlockSpec((tm, LANES), lambda i: (i, 0))),
        compiler_params=_params("parallel"),
        name="rope_tables",
    )(pos, inv, sgn)


def _rope(x, cos_ref, sin_ref):
    w = x.shape[1]
    reps = w // LANES
    cos = jnp.concatenate([cos_ref[...]] * reps, axis=1)
    sin = jnp.concatenate([sin_ref[...]] * reps, axis=1)
    lane = lax.broadcasted_iota(jnp.int32, x.shape, 1)
    first_half = (lane & (HEAD_DIM // 2)) == 0
    swapped = jnp.where(first_half,
                        pltpu.roll(x, w - HEAD_DIM // 2, 1),
                        pltpu.roll(x, HEAD_DIM // 2, 1))
    return x * cos + swapped * sin


def _expand_heads(x):
    pieces = []
    lane = lax.broadcasted_iota(jnp.int32, (x.shape[0], LANES), 1)
    lo = lane < HEAD_DIM
    for c in range(x.shape[1] // LANES):
        a = x[:, c * LANES:(c + 1) * LANES]
        r = pltpu.roll(a, HEAD_DIM, 1)
        pieces += [jnp.where(lo, a, 0.0), jnp.where(lo, 0.0, r),
                   jnp.where(lo, r, 0.0), jnp.where(lo, 0.0, a)]
    return jnp.concatenate(pieces, axis=1)


def _inproj_kernel(h_ref, w_ref, cos_ref, sin_ref, lb_ref,
                   q_ref, k_ref, v_ref, hq_ref, lf_ref, kin_ref, hi_ref, hg_ref):
    j = pl.program_id(1)
    acc = _dot(h_ref[...], w_ref[...])
    ct = COL_TILE

    for c in range(2):
        @pl.when(j == c)
        def _(c=c):
            q_ref[:, c * ct:(c + 1) * ct] = (
                _rope(acc, cos_ref, sin_ref) * (HEAD_DIM ** -0.5)).astype(BF16)

    @pl.when(j == 2)
    def _():
        k_ref[...] = _expand_heads(_rope(acc[:, :KV_WIDTH], cos_ref, sin_ref)).astype(BF16)
        v_ref[...] = _expand_heads(acc[:, KV_WIDTH:]).astype(BF16)

    for c in range(2):
        @pl.when(j == 3 + c)
        def _(c=c):
            hq_ref[:, c * ct:(c + 1) * ct] = (acc * _sigmoid(acc)).astype(BF16)

        @pl.when(j == 5 + c)
        def _(c=c):
            lb = lb_ref[:, c * ct:(c + 1) * ct]
            f = lb + (1.0 - lb) * _sigmoid(acc)
            lf_ref[:, c * ct:(c + 1) * ct] = jnp.log(f)
            kin_ref[:, c * ct:(c + 1) * ct] = (
                (1.0 - lb) * (1.0 / (1.0 + jnp.exp(acc)))).astype(BF16)

        @pl.when(j == 7 + c)
        def _(c=c):
            hi_ref[:, c * ct:(c + 1) * ct] = acc.astype(BF16)

        @pl.when(j == 9 + c)
        def _(c=c):
            hg_ref[:, c * ct:(c + 1) * ct] = (acc * _sigmoid(acc)).astype(BF16)


def _inproj(h, w, cos, sin, lb):
    n = h.shape[0]
    tm = ROW_TILE
    row = lambda w_: pl.BlockSpec((tm, w_), lambda i, j: (i, 0))
    wide = jax.ShapeDtypeStruct((n, 1024), BF16)
    return pl.pallas_call(
        _inproj_kernel,
        out_shape=(wide, wide, wide, wide,
                   jax.ShapeDtypeStruct((n, HGRN_WIDTH), F32), wide, wide, wide),
        grid=(n // tm, D_IN // COL_TILE),
        in_specs=[row(D_MODEL),
                  pl.BlockSpec((D_MODEL, COL_TILE), lambda i, j: (0, j)),
                  row(LANES), row(LANES),
                  pl.BlockSpec((1, HGRN_WIDTH), lambda i, j: (0, 0))],
        out_specs=tuple(row(1024) for _ in range(8)),
        compiler_params=_params("parallel", "arbitrary"),
        name="in_proj",
    )(h, w, cos, sin, lb)


def _attn_kernel(sink_ref, q_ref, kc_ref, kp_ref, vc_ref, vp_ref, g_ref, o_ref):
    n = pl.program_id(1)
    L = WINDOW
    qi = lax.broadcasted_iota(jnp.int32, (2 * L, 2 * L), 0) & (L - 1)
    kj = lax.broadcasted_iota(jnp.int32, (2 * L, 2 * L), 1)
    rel = qi + L - kj
    valid = (rel >= 0) & (rel < L) & ((kj >= L) | (n > 0))
    lower_rows = lax.broadcasted_iota(jnp.int32, (2 * L, 1), 0) >= L

    def softmax_parts(s, sink_top, sink_bot):
        s = jnp.where(valid, s, MASK_VALUE)
        sink = jnp.where(lower_rows, sink_bot, sink_top)
        m = jnp.maximum(jnp.max(s, axis=-1, keepdims=True), sink)
        e = jnp.exp(s - m)
        den = jnp.sum(e, axis=-1, keepdims=True) + jnp.exp(sink - m)
        return e.astype(BF16), 1.0 / den

    outs = []
    for hk in range(N_KV_HEADS):
        lo = slice(2 * LANES * hk, 2 * LANES * hk + LANES)
        hi = slice(2 * LANES * hk + LANES, 2 * LANES * (hk + 1))
        kz0 = jnp.concatenate([kp_ref[:, lo], kc_ref[:, lo]], axis=0)
        kz1 = jnp.concatenate([kp_ref[:, hi], kc_ref[:, hi]], axis=0)
        vz0 = jnp.concatenate([vp_ref[:, lo], vc_ref[:, lo]], axis=0)
        vz1 = jnp.concatenate([vp_ref[:, hi], vc_ref[:, hi]], axis=0)
        lhs = jnp.concatenate([q_ref[:, lo], q_ref[:, hi]], axis=0)
        ea, ra = softmax_parts(_dot_nt(lhs, kz0), sink_ref[4 * hk], sink_ref[4 * hk + 2])
        eb, rb = softmax_parts(_dot_nt(lhs, kz1), sink_ref[4 * hk + 1], sink_ref[4 * hk + 3])
        comb = _dot(ea, vz0) * ra + _dot(eb, vz1) * rb
        outs += [comb[:L], comb[L:]]
    o = jnp.concatenate(outs, axis=1)
    o_ref[...] = _rms(o, g_ref[...]).astype(o_ref.dtype)


def _attention(sinks, q, k, v, gain, batch, seq):
    nb = seq // WINDOW
    q = q.reshape(batch, seq, ATTN_WIDTH)
    k = k.reshape(batch, seq, 1024)
    v = v.reshape(batch, seq, 1024)
    cur = pl.BlockSpec((None, WINDOW, 1024), lambda b, n: (b, n, 0))
    prev = pl.BlockSpec((None, WINDOW, 1024), lambda b, n: (b, jnp.maximum(n - 1, 0), 0))
    out = pl.pallas_call(
        _attn_kernel,
        out_shape=jax.ShapeDtypeStruct((batch, seq, ATTN_WIDTH), BF16),
        grid=(batch, nb),
        in_specs=[pl.BlockSpec(memory_space=pltpu.SMEM),
                  cur, cur, prev, cur, prev,
                  pl.BlockSpec((1, ATTN_WIDTH), lambda b, n: (0, 0))],
        out_specs=cur,
        compiler_params=_params("parallel", "arbitrary"),
        name="swa_attention",
    )(sinks, q, k, k, v, v, gain)
    return out.reshape(batch * seq, ATTN_WIDTH)


def _hgrn_kernel(q_ref, lf_ref, k_ref, v_ref, g_ref, gain_ref, o_ref,
                 st_ref, b_scr, k_scr):
    c_idx = pl.program_id(1)
    C = HGRN_CHUNK

    @pl.when(c_idx == 0)
    def _():
        st_ref[...] = jnp.zeros_like(st_ref)

    row = lax.broadcasted_iota(jnp.int32, (C, C), 0)
    col = lax.broadcasted_iota(jnp.int32, (C, C), 1)
    tri = (row >= col).astype(F32).astype(BF16)
    level_masks = []
    half = C // 2
    while half >= SUBLANES:
        shift = half.bit_length() - 1
        th = row >> shift
        sh = col >> shift
        level_masks.append((half, (th == sh + 1) & ((th & 1) == 1)))
        half //= 2
    sub = lax.broadcasted_iota(jnp.int32, (SUBLANES, LANES), 0)
    lane = lax.broadcasted_iota(jnp.int32, (SUBLANES, LANES), 1)

    for h in range(HGRN_HEADS):
        sl = slice(h * HGRN_DIM, (h + 1) * HGRN_DIM)
        g = lf_ref[:, sl]
        g1 = g.astype(BF16)
        r1 = g - g1.astype(F32)
        g2 = r1.astype(BF16)
        g3 = (r1 - g2.astype(F32)).astype(BF16)
        b = _dot(tri, g1) + _dot(tri, g2) + _dot(tri, g3)
        q = q_ref[:, sl].astype(F32)
        k = k_ref[:, sl].astype(F32)
        v = v_ref[:, sl]
        b_scr[h] = b
        k_scr[h] = k

        a = jnp.zeros((C, C), F32)
        for half, mask in level_masks:
            par = 2 * half
            ref_rows = jnp.concatenate(
                [jnp.broadcast_to(b[p0 + half - 1:p0 + half, :], (par, HGRN_DIM))
                 for p0 in range(0, C, par)], axis=0)
            e = jnp.exp(-jnp.abs(b - ref_rows))
            al = _dot_nt((q * e).astype(BF16), (k * e).astype(BF16))
            a = a + jnp.where(mask, al, 0.0)

        diag = []
        for r in range(C // SUBLANES):
            bb = b[r * SUBLANES:(r + 1) * SUBLANES]
            qb = q[r * SUBLANES:(r + 1) * SUBLANES]
            blk = jnp.zeros((SUBLANES, LANES), F32)
            for s in range(SUBLANES):
                t0 = r * SUBLANES + s
                bs = jnp.broadcast_to(b_scr[h, t0:t0 + 1, :], (SUBLANES, HGRN_DIM))
                ks = jnp.broadcast_to(k_scr[h, t0:t0 + 1, :], (SUBLANES, HGRN_DIM))
                w = jnp.sum(qb * ks * jnp.exp(bb - bs), axis=-1, keepdims=True)
                blk = jnp.where((lane == t0) & (sub >= s), w, blk)
            diag.append(blk)
        a = a + jnp.concatenate(diag, axis=0)

        st = st_ref[h]
        o = _dot(a.astype(BF16), v) + _dot_nt((q * jnp.exp(b)).astype(BF16), st.astype(BF16))
        b_last = b[C - 1:C, :]
        ke = (k * jnp.exp(b_last - b)).astype(BF16)
        st_ref[h] = st * jnp.exp(b_last) + _dot_tn(v, ke)

        y = _rms(o, gain_ref[:, sl]) * g_ref[:, sl].astype(F32)
        o_ref[:, sl] = y.astype(o_ref.dtype)


def _hgrn(q, lf, k, v, g, gain, batch, seq):
    nc = seq // HGRN_CHUNK
    shp = (batch, seq, HGRN_WIDTH)
    blk = pl.BlockSpec((None, HGRN_CHUNK, HGRN_WIDTH), lambda b, c: (b, c, 0))
    out = pl.pallas_call(
        _hgrn_kernel,
        out_shape=jax.ShapeDtypeStruct(shp, BF16),
        grid=(batch, nc),
        in_specs=[blk, blk, blk, blk, blk,
                  pl.BlockSpec((1, HGRN_WIDTH), lambda b, c: (0, 0))],
        out_specs=blk,
        scratch_shapes=[pltpu.VMEM((HGRN_HEADS, HGRN_DIM, HGRN_DIM), F32),
                        pltpu.VMEM((HGRN_HEADS, HGRN_CHUNK, HGRN_DIM), F32),
                        pltpu.VMEM((HGRN_HEADS, HGRN_CHUNK, HGRN_DIM), F32)],
        compiler_params=_params("parallel", "arbitrary"),
        name="hgrn2",
    )(q.reshape(shp), lf.reshape(shp), k.reshape(shp), v.reshape(shp), g.reshape(shp), gain)
    return out.reshape(batch * seq, HGRN_WIDTH)


def _outproj_kernel(a_ref, r_ref, w_ref, x_ref, g1_ref, g2_ref, xo_ref, ho_ref):
    m = _dot(a_ref[...], w_ref[:ATTN_WIDTH, :]) + _dot(r_ref[...], w_ref[ATTN_WIDTH:, :])
    x1 = x_ref[...] + _rms(m, g1_ref[...])
    xo_ref[...] = x1
    ho_ref[...] = _rms(x1, g2_ref[...]).astype(ho_ref.dtype)


def _outproj(attn, hgrn, w, x, g1, g2):
    n = x.shape[0]
    tm = ROW_TILE // 2
    row = lambda w_: pl.BlockSpec((tm, w_), lambda i: (i, 0))
    vec = pl.BlockSpec((1, D_MODEL), lambda i: (0, 0))
    return pl.pallas_call(
        _outproj_kernel,
        out_shape=(jax.ShapeDtypeStruct((n, D_MODEL), F32),
                   jax.ShapeDtypeStruct((n, D_MODEL), BF16)),
        grid=(n // tm,),
        in_specs=[row(ATTN_WIDTH), row(HGRN_WIDTH),
                  pl.BlockSpec((D_MODEL, D_MODEL), lambda i: (0, 0)),
                  row(D_MODEL), vec, vec],
        out_specs=(row(D_MODEL), row(D_MODEL)),
        compiler_params=_params("parallel"),
        name="out_proj",
    )(attn, hgrn, w, x, g1, g2)


def _ffn_kernel(h_ref, wg_ref, wu_ref, wd_ref, x_ref, g1_ref, g2_ref,
                xo_ref, ho_ref, acc_ref):
    j = pl.program_id(1)

    @pl.when(j == 0)
    def _():
        acc_ref[...] = jnp.zeros_like(acc_ref)

    h = h_ref[...]
    gate = _dot(h, wg_ref[...])
    up = _dot(h, wu_ref[...])
    act = (gate * _sigmoid(gate) * up).astype(BF16)
    acc_ref[...] += _dot(act, wd_ref[...])

    @pl.when(j == pl.num_programs(1) - 1)
    def _():
        x2 = x_ref[...] + _rms(acc_ref[...], g1_ref[...])
        xo_ref[...] = x2
        ho_ref[...] = _rms(x2, g2_ref[...]).astype(ho_ref.dtype)


def _ffn(h, wg, wu, wd, x, g1, g2):
    n = x.shape[0]
    tm = ROW_TILE
    tf = COL_TILE
    row = lambda w_: pl.BlockSpec((tm, w_), lambda i, j: (i, 0))
    vec = pl.BlockSpec((1, D_MODEL), lambda i, j: (0, 0))
    return pl.pallas_call(
        _ffn_kernel,
        out_shape=(jax.ShapeDtypeStruct((n, D_MODEL), F32),
                   jax.ShapeDtypeStruct((n, D_MODEL), BF16)),
        grid=(n // tm, D_FF // tf),
        in_specs=[row(D_MODEL),
                  pl.BlockSpec((D_MODEL, tf), lambda i, j: (0, j)),
                  pl.BlockSpec((D_MODEL, tf), lambda i, j: (0, j)),
                  pl.BlockSpec((tf, D_MODEL), lambda i, j: (j, 0)),
                  row(D_MODEL), vec, vec],
        out_specs=(row(D_MODEL), row(D_MODEL)),
        scratch_shapes=[pltpu.VMEM((tm, D_MODEL), F32)],
        compiler_params=_params("parallel", "arbitrary"),
        name="ffn",
    )(h, wg, wu, wd, x, g1, g2)


def _ple_kernel(h_ref, wg_ref, p_ref, wp_ref, x_ref, g_ref, xo_ref, ho_ref):
    gate = _sigmoid(_dot(h_ref[...], wg_ref[...]))
    proj = _dot(p_ref[...].astype(BF16), wp_ref[...])
    x3 = x_ref[...] + proj * gate
    xo_ref[...] = x3
    ho_ref[...] = _rms(x3, g_ref[...]).astype(ho_ref.dtype)


def _ple(h, wg, p, wp, x, g_next):
    n = x.shape[0]
    tm = ROW_TILE // 2
    row = lambda w_: pl.BlockSpec((tm, w_), lambda i: (i, 0))
    return pl.pallas_call(
        _ple_kernel,
        out_shape=(jax.ShapeDtypeStruct((n, D_MODEL), F32),
                   jax.ShapeDtypeStruct((n, D_MODEL), BF16)),
        grid=(n // tm,),
        in_specs=[row(D_MODEL),
                  pl.BlockSpec((D_MODEL, D_MODEL), lambda i: (0, 0)),
                  row(D_PLE),
                  pl.BlockSpec((D_PLE, D_MODEL), lambda i: (0, 0)),
                  row(D_MODEL),
                  pl.BlockSpec((1, D_MODEL), lambda i: (0, 0))],
        out_specs=(row(D_MODEL), row(D_MODEL)),
        compiler_params=_params("parallel"),
        name="ple",
    )(h, wg, p, wp, x, g_next)


def kernel(x, p, positions, w_in, attn_sinks, hgrn_lb_logits, attn_out_gain, hgrn_out_gain,
           w_out, pre_mix_gain, post_mix_gain, pre_ffn_gain, post_ffn_gain,
           w_ffn_gate, w_ffn_up, w_ffn_down, ple_gain, w_ple_gate, w_ple_proj):
    batch, seq, d = x.shape
    n = batch * seq
    depth = w_in.shape[0]
    half = HEAD_DIM // 2

    lb_soft = jax.nn.softmax(hgrn_lb_logits.astype(F32), axis=0)
    lower_bounds = jnp.cumsum(lb_soft, axis=0) - lb_soft[0:1]

    inv_freq = ROPE_THETA ** (-jnp.arange(half, dtype=F32) / half)
    inv = jnp.tile(inv_freq, LANES // half).reshape(1, LANES)
    sgn = jnp.tile(jnp.concatenate([-jnp.ones((half,), F32), jnp.ones((half,), F32)]),
                   LANES // HEAD_DIM).reshape(1, LANES)
    cos, sin = _rope_tables(positions.reshape(n, 1), inv, sgn)

    vec = lambda a: a.astype(F32).reshape(1, -1)
    xf = x.reshape(n, d).astype(F32)
    h = _norm(xf, vec(pre_mix_gain[0]))
    for i in range(depth):
        q, k, v, hq, lf, kin, hi, hg = _inproj(
            h, w_in[i].astype(BF16), cos, sin, lower_bounds[i].reshape(1, -1))
        attn = _attention(attn_sinks[i].astype(F32), q, k, v, vec(attn_out_gain[i]), batch, seq)
        hgrn = _hgrn(hq, lf, kin, hi, hg, vec(hgrn_out_gain[i]), batch, seq)
        xf, h = _outproj(attn, hgrn, w_out[i].astype(BF16), xf,
                         vec(post_mix_gain[i]), vec(pre_ffn_gain[i]))
        xf, h = _ffn(h, w_ffn_gate[i].astype(BF16), w_ffn_up[i].astype(BF16),
                     w_ffn_down[i].astype(BF16), xf,
                     vec(post_ffn_gain[i]), vec(ple_gain[i]))
        g_next = pre_mix_gain[i + 1] if i + 1 < depth else pre_mix_gain[i]
        xf, h = _ple(h, w_ple_gate[i].astype(BF16), p[i].reshape(n, D_PLE),
                     w_ple_proj[i].astype(BF16), xf, vec(g_next))
    return xf.reshape(batch, seq, d).astype(x.dtype)
```

```python
import jax
import jax.numpy as jnp
from jax import lax
from jax.experimental import pallas as pl
from jax.experimental.pallas import tpu as pltpu

F32 = jnp.float32
BF16 = jnp.bfloat16

D_MODEL = 2048
ATTN_WIDTH = 1024
HGRN_WIDTH = 1024
HEAD_DIM = 64
N_Q_HEADS = 16
N_KV_HEADS = 4
WINDOW = 128
ROPE_THETA = 10000.0
MASK_VALUE = -1e30
HGRN_HEADS = 8
HGRN_DIM = 128
D_FF = 5632
D_PLE = 256
RMS_EPS = 1e-6
KV_WIDTH = N_KV_HEADS * HEAD_DIM
LOG2_E = 1.4426950408889634

LANES = 128
SUBLANES = 8
VMEM_LIMIT = 56 * 1024 * 1024

ROW_TILE = 512
ROW_SUB = 256
COL_TILE = 512
GROUP_TILE = 1024
SUB_TILE = 256
HGRN_CHUNK = 128


def _params(*sem):
    return pltpu.CompilerParams(dimension_semantics=sem, vmem_limit_bytes=VMEM_LIMIT)


def _rms(x, gain):
    ms = jnp.mean(x * x, axis=-1, keepdims=True)
    return x * lax.rsqrt(ms + RMS_EPS) * gain


def _sigmoid(x):
    return 1.0 / (1.0 + jnp.exp(-x))


def _dot(a, b):
    return jnp.dot(a, b, preferred_element_type=F32)


def _dot_nt(a, b):
    return lax.dot_general(a, b, (((1,), (1,)), ((), ())), preferred_element_type=F32)


def _dot_tn(a, b):
    return lax.dot_general(a, b, (((0,), (0,)), ((), ())), preferred_element_type=F32)


def _norm_kernel(x_ref, g_ref, o_ref):
    o_ref[...] = _rms(x_ref[...], g_ref[...]).astype(o_ref.dtype)


def _norm(x, gain, layer):
    n, d = x.shape
    return pl.pallas_call(
        _norm_kernel,
        out_shape=jax.ShapeDtypeStruct((n, d), BF16),
        grid=(n // ROW_TILE,),
        in_specs=[pl.BlockSpec((ROW_TILE, d), lambda i: (i, 0)),
                  pl.BlockSpec((None, 1, d), lambda i: (layer, 0, 0))],
        out_specs=pl.BlockSpec((ROW_TILE, d), lambda i: (i, 0)),
        compiler_params=_params("parallel"),
        name="pre_norm",
    )(x, gain)


def _rope_table_kernel(pos_ref, inv_ref, sgn_ref, cos_ref, sin_ref):
    ang = pos_ref[...].astype(F32) * inv_ref[...]
    cos_ref[...] = jnp.cos(ang)
    sin_ref[...] = jnp.sin(ang) * sgn_ref[...]


def _rope_tables(pos, inv, sgn):
    n = pos.shape[0]
    tm = 1024
    return pl.pallas_call(
        _rope_table_kernel,
        out_shape=(jax.ShapeDtypeStruct((n, LANES), F32),
                   jax.ShapeDtypeStruct((n, LANES), F32)),
        grid=(n // tm,),
        in_specs=[pl.BlockSpec((tm, 1), lambda i: (i, 0)),
                  pl.BlockSpec((1, LANES), lambda i: (0, 0)),
                  pl.BlockSpec((1, LANES), lambda i: (0, 0))],
        out_specs=(pl.BlockSpec((tm, LANES), lambda i: (i, 0)),
                   pl.BlockSpec((tm, LANES), lambda i: (i, 0))),
        compiler_params=_params("parallel"),
        name="rope_tables",
    )(pos, inv, sgn)


def _rope(x, cos, sin):
    parts = []
    for c in range(x.shape[1] // LANES):
        xc = x[:, c * LANES:(c + 1) * LANES]
        parts.append(xc * cos + pltpu.roll(xc, LANES // 2, 1) * sin)
    return jnp.concatenate(parts, axis=1)


def _expand_k(x):
    group = lax.broadcasted_iota(jnp.int32, (x.shape[0], LANES), 1) >> 5
    first = (group & 1) == 0
    pieces = []
    for c in range(x.shape[1] // LANES):
        xc = x[:, c * LANES:(c + 1) * LANES]
        a0 = jnp.where(first, xc, 0.0)
        b1 = jnp.where(first, 0.0, xc)
        pieces += [a0, pltpu.roll(a0, HEAD_DIM // 2, 1),
                   pltpu.roll(b1, LANES - HEAD_DIM // 2, 1), b1]
    return jnp.concatenate(pieces, axis=1)


def _expand_v(x):
    pieces = []
    lo = lax.broadcasted_iota(jnp.int32, (x.shape[0], LANES), 1) < HEAD_DIM
    for c in range(x.shape[1] // LANES):
        a = x[:, c * LANES:(c + 1) * LANES]
        r = pltpu.roll(a, HEAD_DIM, 1)
        pieces += [jnp.where(lo, a, 0.0), jnp.where(lo, 0.0, r),
                   jnp.where(lo, r, 0.0), jnp.where(lo, 0.0, a)]
    return jnp.concatenate(pieces, axis=1)


def _inproj_kernel(h_ref, w_ref, wkv_ref, cos_ref, sin_ref, lb_ref,
                   q_ref, k_ref, v_ref, hq_ref, lf_ref, kin_ref, hi_ref, hg_ref):
    j = pl.program_id(1)
    st = SUB_TILE
    n_sub = w_ref.shape[1] // st

    def sub_dot(w, c):
        return _dot(h_ref[...], w[:, c * st:(c + 1) * st])

    @pl.when(j == 0)
    def _():
        cos = cos_ref[...]
        sin = sin_ref[...]
        for c in range(n_sub):
            q_ref[:, c * st:(c + 1) * st] = (
                _rope(sub_dot(w_ref, c), cos, sin) * (HEAD_DIM ** -0.5)).astype(BF16)
        k_ref[...] = _expand_k(_rope(sub_dot(wkv_ref, 0), cos, sin)).astype(BF16)
        v_ref[...] = _expand_v(sub_dot(wkv_ref, 1)).astype(BF16)

    def silu_group(o_ref):
        for c in range(n_sub):
            acc = sub_dot(w_ref, c)
            o_ref[:, c * st:(c + 1) * st] = (acc * _sigmoid(acc)).astype(BF16)

    @pl.when(j == 1)
    def _():
        silu_group(hq_ref)

    @pl.when(j == 2)
    def _():
        for c in range(n_sub):
            acc = sub_dot(w_ref, c)
            lb = lb_ref[:, c * st:(c + 1) * st]
            s = _sigmoid(acc)
            lf_ref[:, c * st:(c + 1) * st] = jnp.log(lb + (1.0 - lb) * s)
            kin_ref[:, c * st:(c + 1) * st] = ((1.0 - lb) * (1.0 - s)).astype(BF16)

    @pl.when(j == 3)
    def _():
        for c in range(n_sub):
            hi_ref[:, c * st:(c + 1) * st] = sub_dot(w_ref, c).astype(BF16)

    @pl.when(j == 4)
    def _():
        silu_group(hg_ref)


def _inproj(h, w, wkv, layer, cos, sin, lb):
    n = h.shape[0]
    tm = ROW_TILE
    tn = GROUP_TILE
    row = lambda w_: pl.BlockSpec((tm, w_), lambda i, j: (i, 0))
    wide = jax.ShapeDtypeStruct((n, 1024), BF16)
    return pl.pallas_call(
        _inproj_kernel,
        out_shape=(wide, wide, wide, wide,
                   jax.ShapeDtypeStruct((n, HGRN_WIDTH), F32), wide, wide, wide),
        grid=(n // tm, w.shape[2] // tn),
        in_specs=[row(D_MODEL),
                  pl.BlockSpec((None, D_MODEL, tn), lambda i, j: (layer, 0, j)),
                  pl.BlockSpec((None, D_MODEL, 2 * KV_WIDTH), lambda i, j: (layer, 0, 0)),
                  row(LANES), row(LANES),
                  pl.BlockSpec((None, 1, HGRN_WIDTH), lambda i, j: (layer, 0, 0))],
        out_specs=tuple(row(1024) for _ in range(8)),
        compiler_params=_params("parallel", "arbitrary"),
        name="in_proj",
    )(h, w, wkv, cos, sin, lb)


def _attn_kernel(sink_ref, q_ref, kc_ref, kp_ref, vc_ref, vp_ref, g_ref, o_ref, *, layer):
    n = pl.program_id(1)
    L = WINDOW
    qi = lax.broadcasted_iota(jnp.int32, (2 * L, 2 * L), 0) & (L - 1)
    kj = lax.broadcasted_iota(jnp.int32, (2 * L, 2 * L), 1)
    rel = qi + L - kj
    valid = (rel >= 0) & (rel < L) & ((kj >= L) | (n > 0))
    lower_rows = lax.broadcasted_iota(jnp.int32, (2 * L, 1), 0) >= L

    def softmax_parts(s, sink_top, sink_bot):
        s = jnp.where(valid, s, MASK_VALUE)
        sink = jnp.where(lower_rows, sink_bot, sink_top)
        m = jnp.maximum(jnp.max(s, axis=-1, keepdims=True), sink)
        e = jnp.exp(s - m)
        den = jnp.sum(e, axis=-1, keepdims=True) + jnp.exp(sink - m)
        return e.astype(BF16), 1.0 / den

    outs = []
    for hk in range(N_KV_HEADS):
        lo = slice(2 * LANES * hk, 2 * LANES * hk + LANES)
        hi = slice(2 * LANES * hk + LANES, 2 * LANES * (hk + 1))
        kz0 = jnp.concatenate([kp_ref[:, lo], kc_ref[:, lo]], axis=0)
        kz1 = jnp.concatenate([kp_ref[:, hi], kc_ref[:, hi]], axis=0)
        vz0 = jnp.concatenate([vp_ref[:, lo], vc_ref[:, lo]], axis=0)
        vz1 = jnp.concatenate([vp_ref[:, hi], vc_ref[:, hi]], axis=0)
        lhs = jnp.concatenate([q_ref[:, lo], q_ref[:, hi]], axis=0)
        ea, ra = softmax_parts(_dot_nt(lhs, kz0),
                               sink_ref[layer, 4 * hk], sink_ref[layer, 4 * hk + 2])
        eb, rb = softmax_parts(_dot_nt(lhs, kz1),
                               sink_ref[layer, 4 * hk + 1], sink_ref[layer, 4 * hk + 3])
        comb = _dot(ea, vz0) * ra + _dot(eb, vz1) * rb
        outs += [comb[:L], comb[L:]]
    o = jnp.concatenate(outs, axis=1)
    o_ref[...] = _rms(o, g_ref[...]).astype(o_ref.dtype)


def _attention(sinks, q, k, v, gain, layer, batch, seq):
    nb = seq // WINDOW
    q = q.reshape(batch, seq, ATTN_WIDTH)
    k = k.reshape(batch, seq, 1024)
    v = v.reshape(batch, seq, 1024)
    cur = pl.BlockSpec((None, WINDOW, 1024), lambda b, n: (b, n, 0))
    prev = pl.BlockSpec((None, WINDOW, 1024), lambda b, n: (b, jnp.maximum(n - 1, 0), 0))
    out = pl.pallas_call(
        lambda *refs: _attn_kernel(*refs, layer=layer),
        out_shape=jax.ShapeDtypeStruct((batch, seq, ATTN_WIDTH), BF16),
        grid=(batch, nb),
        in_specs=[pl.BlockSpec(memory_space=pltpu.SMEM),
                  cur, cur, prev, cur, prev,
                  pl.BlockSpec((None, 1, ATTN_WIDTH), lambda b, n: (layer, 0, 0))],
        out_specs=cur,
        compiler_params=_params("parallel", "arbitrary"),
        name="swa_attention",
    )(sinks, q, k, k, v, v, gain)
    return out.reshape(batch * seq, ATTN_WIDTH)


def _hgrn_kernel(q_ref, lf_ref, k_ref, v_ref, g_ref, gain_ref, o_ref,
                 st_ref, b_scr, k_scr):
    c_idx = pl.program_id(1)
    C = HGRN_CHUNK

    @pl.when(c_idx == 0)
    def _():
        st_ref[...] = jnp.zeros_like(st_ref)

    row = lax.broadcasted_iota(jnp.int32, (C, C), 0)
    col = lax.broadcasted_iota(jnp.int32, (C, C), 1)
    tri = (row >= col).astype(F32).astype(BF16)
    level_masks = []
    half = C // 2
    while half >= SUBLANES:
        shift = half.bit_length() - 1
        th = row >> shift
        sh = col >> shift
        level_masks.append((half, (th == sh + 1) & ((th & 1) == 1)))
        half //= 2
    sub = lax.broadcasted_iota(jnp.int32, (SUBLANES, LANES), 0)
    lane = lax.broadcasted_iota(jnp.int32, (SUBLANES, LANES), 1)

    for h in range(HGRN_HEADS):
        sl = slice(h * HGRN_DIM, (h + 1) * HGRN_DIM)
        g = lf_ref[:, sl] * LOG2_E
        g1 = g.astype(BF16)
        r1 = g - g1.astype(F32)
        g2 = r1.astype(BF16)
        g3 = (r1 - g2.astype(F32)).astype(BF16)
        b = _dot(tri, g1) + _dot(tri, g2) + _dot(tri, g3)
        q = q_ref[:, sl].astype(F32)
        k = k_ref[:, sl].astype(F32)
        v = v_ref[:, sl]
        b_scr[h] = b
        k_scr[h] = k

        a = jnp.zeros((C, C), F32)
        for half, mask in level_masks:
            d = []
            for p0 in range(0, C, 2 * half):
                r = b[p0 + half - 1:p0 + half, :]
                d += [r - b[p0:p0 + half], b[p0 + half:p0 + 2 * half] - r]
            e = jnp.exp2(jnp.concatenate(d, axis=0))
            al = _dot_nt((q * e).astype(BF16), (k * e).astype(BF16))
            a = jnp.where(mask, al, a)

        diag = []
        for r in range(C // SUBLANES):
            bb = b[r * SUBLANES:(r + 1) * SUBLANES]
            qb = q[r * SUBLANES:(r + 1) * SUBLANES]
            blk = jnp.zeros((SUBLANES, LANES), F32)
            for s in range(SUBLANES):
                t0 = r * SUBLANES + s
                bs = jnp.broadcast_to(b_scr[h, t0:t0 + 1, :], (SUBLANES, HGRN_DIM))
                ks = jnp.broadcast_to(k_scr[h, t0:t0 + 1, :], (SUBLANES, HGRN_DIM))
                w = jnp.sum(qb * ks * jnp.exp2(bb - bs), axis=-1, keepdims=True)
                blk = jnp.where((lane == t0) & (sub >= s), w, blk)
            diag.append(blk)
        a = a + jnp.concatenate(diag, axis=0)

        st = st_ref[h]
        o = _dot(a.astype(BF16), v) + _dot_nt((q * jnp.exp2(b)).astype(BF16), st.astype(BF16))
        b_last = b[C - 1:C, :]
        ke = (k * jnp.exp2(b_last - b)).astype(BF16)
        st_ref[h] = st * jnp.exp2(b_last) + _dot_tn(v, ke)

        y = _rms(o, gain_ref[:, sl]) * g_ref[:, sl].astype(F32)
        o_ref[:, sl] = y.astype(o_ref.dtype)


def _hgrn(q, lf, k, v, g, gain, layer, batch, seq):
    nc = seq // HGRN_CHUNK
    shp = (batch, seq, HGRN_WIDTH)
    blk = pl.BlockSpec((None, HGRN_CHUNK, HGRN_WIDTH), lambda b, c: (b, c, 0))
    out = pl.pallas_call(
        _hgrn_kernel,
        out_shape=jax.ShapeDtypeStruct(shp, BF16),
        grid=(batch, nc),
        in_specs=[blk, blk, blk, blk, blk,
                  pl.BlockSpec((None, 1, HGRN_WIDTH), lambda b, c: (layer, 0, 0))],
        out_specs=blk,
        scratch_shapes=[pltpu.VMEM((HGRN_HEADS, HGRN_DIM, HGRN_DIM), F32),
                        pltpu.VMEM((HGRN_HEADS, HGRN_CHUNK, HGRN_DIM), F32),
                        pltpu.VMEM((HGRN_HEADS, HGRN_CHUNK, HGRN_DIM), F32)],
        compiler_params=_params("parallel", "arbitrary"),
        name="hgrn2",
    )(q.reshape(shp), lf.reshape(shp), k.reshape(shp), v.reshape(shp), g.reshape(shp), gain)
    return out.reshape(batch * seq, HGRN_WIDTH)


def _outproj_kernel(a_ref, r_ref, w_ref, x_ref, g1_ref, g2_ref, xo_ref, ho_ref):
    for r0 in range(0, a_ref.shape[0], ROW_SUB):
        rows = slice(r0, r0 + ROW_SUB)
        m = (_dot(a_ref[rows, :], w_ref[:ATTN_WIDTH, :])
             + _dot(r_ref[rows, :], w_ref[ATTN_WIDTH:, :]))
        x1 = x_ref[rows, :] + _rms(m, g1_ref[...])
        xo_ref[rows, :] = x1
        ho_ref[rows, :] = _rms(x1, g2_ref[...]).astype(ho_ref.dtype)


def _outproj(attn, hgrn, w, x, g1, g2, layer):
    n = x.shape[0]
    tm = ROW_TILE
    row = lambda w_: pl.BlockSpec((tm, w_), lambda i: (i, 0))
    vec = pl.BlockSpec((None, 1, D_MODEL), lambda i: (layer, 0, 0))
    return pl.pallas_call(
        _outproj_kernel,
        out_shape=(jax.ShapeDtypeStruct((n, D_MODEL), F32),
                   jax.ShapeDtypeStruct((n, D_MODEL), BF16)),
        grid=(n // tm,),
        in_specs=[row(ATTN_WIDTH), row(HGRN_WIDTH),
                  pl.BlockSpec((None, D_MODEL, D_MODEL), lambda i: (layer, 0, 0)),
                  row(D_MODEL), vec, vec],
        out_specs=(row(D_MODEL), row(D_MODEL)),
        compiler_params=_params("parallel"),
        name="out_proj",
    )(attn, hgrn, w, x, g1, g2)


def _ffn_kernel(h_ref, wg_ref, wu_ref, wd_ref, x_ref, g1_ref, g2_ref,
                xo_ref, ho_ref, acc_ref):
    j = pl.program_id(1)
    last = pl.num_programs(1) - 1

    def partial(rows):
        h = h_ref[rows, :]
        gate = _dot(h, wg_ref[...])
        up = _dot(h, wu_ref[...])
        act = (gate * _sigmoid(gate) * up).astype(BF16)
        return _dot(act, wd_ref[...])

    every_row = slice(0, h_ref.shape[0])

    @pl.when(j == 0)
    def _():
        acc_ref[...] = partial(every_row)

    @pl.when((j > 0) & (j < last))
    def _():
        acc_ref[...] += partial(every_row)

    @pl.when(j == last)
    def _():
        for r0 in range(0, h_ref.shape[0], ROW_SUB):
            rows = slice(r0, r0 + ROW_SUB)
            x2 = x_ref[rows, :] + _rms(acc_ref[rows, :] + partial(rows), g1_ref[...])
            xo_ref[rows, :] = x2
            ho_ref[rows, :] = _rms(x2, g2_ref[...]).astype(ho_ref.dtype)


def _ffn(h, wg, wu, wd, x, g1, g2, layer):
    n = x.shape[0]
    tm = ROW_TILE
    tf = COL_TILE
    row = lambda w_: pl.BlockSpec((tm, w_), lambda i, j: (i, 0))
    vec = pl.BlockSpec((None, 1, D_MODEL), lambda i, j: (layer, 0, 0))
    return pl.pallas_call(
        _ffn_kernel,
        out_shape=(jax.ShapeDtypeStruct((n, D_MODEL), F32),
                   jax.ShapeDtypeStruct((n, D_MODEL), BF16)),
        grid=(n // tm, D_FF // tf),
        in_specs=[row(D_MODEL),
                  pl.BlockSpec((None, D_MODEL, tf), lambda i, j: (layer, 0, j)),
                  pl.BlockSpec((None, D_MODEL, tf), lambda i, j: (layer, 0, j)),
                  pl.BlockSpec((None, tf, D_MODEL), lambda i, j: (layer, j, 0)),
                  row(D_MODEL), vec, vec],
        out_specs=(row(D_MODEL), row(D_MODEL)),
        scratch_shapes=[pltpu.VMEM((tm, D_MODEL), F32)],
        compiler_params=_params("parallel", "arbitrary"),
        name="ffn",
    )(h, wg, wu, wd, x, g1, g2)


def _ple_kernel(h_ref, wg_ref, p_ref, wp_ref, x_ref, *rest):
    for r0 in range(0, h_ref.shape[0], ROW_SUB):
        rows = slice(r0, r0 + ROW_SUB)
        gate = _sigmoid(_dot(h_ref[rows, :], wg_ref[...]))
        proj = _dot(p_ref[rows, :].astype(BF16), wp_ref[...])
        x3 = x_ref[rows, :] + proj * gate
        if len(rest) == 1:
            rest[0][rows, :] = x3
        else:
            g_ref, xo_ref, ho_ref = rest
            xo_ref[rows, :] = x3
            ho_ref[rows, :] = _rms(x3, g_ref[...]).astype(ho_ref.dtype)


def _ple(h, wg, p, wp, x, gains, layer, last):
    n = x.shape[0]
    tm = ROW_TILE
    nb = n // tm
    row = lambda w_: pl.BlockSpec((tm, w_), lambda i: (i, 0))
    in_specs = [row(D_MODEL),
                pl.BlockSpec((None, D_MODEL, D_MODEL), lambda i: (layer, 0, 0)),
                pl.BlockSpec((None, tm, D_PLE), lambda i: (layer, i, 0)),
                pl.BlockSpec((None, D_PLE, D_MODEL), lambda i: (layer, 0, 0)),
                row(D_MODEL)]
    x_shape = jax.ShapeDtypeStruct((n, D_MODEL), F32)
    args = (h, wg, p, wp, x)
    if last:
        out_shape, out_specs = x_shape, row(D_MODEL)
    else:
        in_specs.append(pl.BlockSpec((None, 1, D_MODEL), lambda i: (layer + 1, 0, 0)))
        args += (gains,)
        out_shape = (x_shape, jax.ShapeDtypeStruct((n, D_MODEL), BF16))
        out_specs = (row(D_MODEL), row(D_MODEL))
    out = pl.pallas_call(
        _ple_kernel,
        out_shape=out_shape,
        grid=(nb,),
        in_specs=in_specs,
        out_specs=out_specs,
        compiler_params=_params("parallel"),
        name="ple",
    )(*args)
    return (out, None) if last else out


def _pair_interleave(w):
    lead = w.shape[:-1]
    half = HEAD_DIM // 2
    w = w.reshape(*lead, -1, 2, 2, half)
    return jnp.swapaxes(w, -2, -3).reshape(*lead, -1)


def kernel(x, p, positions, w_in, attn_sinks, hgrn_lb_logits, attn_out_gain, hgrn_out_gain,
           w_out, pre_mix_gain, post_mix_gain, pre_ffn_gain, post_ffn_gain,
           w_ffn_gate, w_ffn_up, w_ffn_down, ple_gain, w_ple_gate, w_ple_proj):
    batch, seq, d = x.shape
    n = batch * seq
    depth = w_in.shape[0]
    half = HEAD_DIM // 2

    lb_soft = jax.nn.softmax(hgrn_lb_logits.astype(F32), axis=0)
    lower_bounds = (jnp.cumsum(lb_soft, axis=0) - lb_soft[0:1]).reshape(depth, 1, HGRN_WIDTH)

    inv_freq = ROPE_THETA ** (-jnp.arange(half, dtype=F32) / half)
    inv = jnp.tile(inv_freq, LANES // half).reshape(1, LANES)
    sgn = jnp.concatenate([-jnp.ones((LANES // 2,), F32),
                           jnp.ones((LANES // 2,), F32)]).reshape(1, LANES)
    cos, sin = _rope_tables(positions.reshape(n, 1), inv, sgn)

    kv0 = ATTN_WIDTH
    rest0 = ATTN_WIDTH + 2 * KV_WIDTH
    w_main = jnp.concatenate(
        [_pair_interleave(w_in[:, :, :kv0]), w_in[:, :, rest0:]], axis=2).astype(BF16)
    w_kv = jnp.concatenate(
        [_pair_interleave(w_in[:, :, kv0:kv0 + KV_WIDTH]), w_in[:, :, kv0 + KV_WIDTH:rest0]],
        axis=2).astype(BF16)
    w_out_b = w_out.astype(BF16)
    w_gate_b = w_ffn_gate.astype(BF16)
    w_up_b = w_ffn_up.astype(BF16)
    w_down_b = w_ffn_down.astype(BF16)
    w_pg_b = w_ple_gate.astype(BF16)
    w_pp_b = w_ple_proj.astype(BF16)
    p2 = p.reshape(depth, n, D_PLE)

    vecs = lambda a: a.astype(F32).reshape(depth, 1, -1)
    pre_mix, post_mix = vecs(pre_mix_gain), vecs(post_mix_gain)
    pre_ffn, post_ffn = vecs(pre_ffn_gain), vecs(post_ffn_gain)
    ple_g, attn_g, hgrn_g = vecs(ple_gain), vecs(attn_out_gain), vecs(hgrn_out_gain)
    sinks = attn_sinks.astype(F32)

    xf = x.reshape(n, d).astype(F32)
    h = _norm(xf, pre_mix, 0)
    for i in range(depth):
        q, k, v, hq, lf, kin, hi, hg = _inproj(h, w_main, w_kv, i, cos, sin, lower_bounds)
        attn = _attention(sinks, q, k, v, attn_g, i, batch, seq)
        hgrn = _hgrn(hq, lf, kin, hi, hg, hgrn_g, i, batch, seq)
        xf, h = _outproj(attn, hgrn, w_out_b, xf, post_mix, pre_ffn, i)
        xf, h = _ffn(h, w_gate_b, w_up_b, w_down_b, xf, post_ffn, ple_g, i)
        xf, h = _ple(h, w_pg_b, p2, w_pp_b, xf, pre_mix, i, last=(i + 1 == depth))
    return xf.reshape(batch, seq, d).astype(x.dtype)
```

```python
import jax
import jax.numpy as jnp
from jax import lax
from jax.experimental import pallas as pl
from jax.experimental.pallas import tpu as pltpu

F32 = jnp.float32
BF16 = jnp.bfloat16

D_MODEL = 2048
ATTN_WIDTH = 1024
HGRN_WIDTH = 1024
HEAD_DIM = 64
N_Q_HEADS = 16
N_KV_HEADS = 4
WINDOW = 128
ROPE_THETA = 10000.0
MASK_VALUE = -1e30
HGRN_HEADS = 8
HGRN_DIM = 128
D_FF = 5632
D_PLE = 256
RMS_EPS = 1e-6
KV_WIDTH = N_KV_HEADS * HEAD_DIM
LOG2_E = 1.4426950408889634

LANES = 128
SUBLANES = 8
VMEM_LIMIT = 56 * 1024 * 1024
MIXER_VMEM_LIMIT = 61 * 1024 * 1024

ROW_TILE = 512
ROW_SUB = 256
COL_TILE = 512
SUB_TILE = 256
HGRN_CHUNK = 128


def _params(*sem):
    return pltpu.CompilerParams(dimension_semantics=sem, vmem_limit_bytes=VMEM_LIMIT)


def _rms(x, gain):
    ms = jnp.mean(x * x, axis=-1, keepdims=True)
    return x * lax.rsqrt(ms + RMS_EPS) * gain


def _sigmoid(x):
    return 1.0 / (1.0 + jnp.exp(-x))


def _dot(a, b):
    return jnp.dot(a, b, preferred_element_type=F32)


def _dot_nt(a, b):
    return lax.dot_general(a, b, (((1,), (1,)), ((), ())), preferred_element_type=F32)


def _dot_tn(a, b):
    return lax.dot_general(a, b, (((0,), (0,)), ((), ())), preferred_element_type=F32)


def _norm_kernel(x_ref, g_ref, o_ref):
    o_ref[...] = _rms(x_ref[...], g_ref[...]).astype(o_ref.dtype)


def _norm(x, gain, layer):
    n, d = x.shape
    return pl.pallas_call(
        _norm_kernel,
        out_shape=jax.ShapeDtypeStruct((n, d), BF16),
        grid=(n // ROW_TILE,),
        in_specs=[pl.BlockSpec((ROW_TILE, d), lambda i: (i, 0)),
                  pl.BlockSpec((None, 1, d), lambda i: (layer, 0, 0))],
        out_specs=pl.BlockSpec((ROW_TILE, d), lambda i: (i, 0)),
        compiler_params=_params("parallel"),
        name="pre_norm",
    )(x, gain)


def _rope_table_kernel(pos_ref, inv_ref, sgn_ref, cos_ref, sin_ref):
    ang = pos_ref[...].astype(F32) * inv_ref[...]
    cos_ref[...] = jnp.cos(ang)
    sin_ref[...] = jnp.sin(ang) * sgn_ref[...]


def _rope_tables(pos, inv, sgn):
    n = pos.shape[0]
    tm = 1024
    return pl.pallas_call(
        _rope_table_kernel,
        out_shape=(jax.ShapeDtypeStruct((n, LANES), F32),
                   jax.ShapeDtypeStruct((n, LANES), F32)),
        grid=(n // tm,),
        in_specs=[pl.BlockSpec((tm, 1), lambda i: (i, 0)),
                  pl.BlockSpec((1, LANES), lambda i: (0, 0)),
                  pl.BlockSpec((1, LANES), lambda i: (0, 0))],
        out_specs=(pl.BlockSpec((tm, LANES), lambda i: (i, 0)),
                   pl.BlockSpec((tm, LANES), lambda i: (i, 0))),
        compiler_params=_params("parallel"),
        name="rope_tables",
    )(pos, inv, sgn)


QKV_COLS = ATTN_WIDTH + 2 * KV_WIDTH
PAIR_COLS = 4 * 2 * HGRN_DIM


def _rope(x, cos, sin):
    parts = []
    for c in range(x.shape[1] // LANES):
        xc = x[:, c * LANES:(c + 1) * LANES]
        parts.append(xc * cos + pltpu.roll(xc, LANES // 2, 1) * sin)
    return jnp.concatenate(parts, axis=1)


def _expand_k(x):
    group = lax.broadcasted_iota(jnp.int32, (x.shape[0], LANES), 1) >> 5
    first = (group & 1) == 0
    pieces = []
    for c in range(x.shape[1] // LANES):
        xc = x[:, c * LANES:(c + 1) * LANES]
        a0 = jnp.where(first, xc, 0.0)
        b1 = jnp.where(first, 0.0, xc)
        pieces += [a0, pltpu.roll(a0, HEAD_DIM // 2, 1),
                   pltpu.roll(b1, LANES - HEAD_DIM // 2, 1), b1]
    return jnp.concatenate(pieces, axis=1)


def _expand_v(x):
    pieces = []
    lo = lax.broadcasted_iota(jnp.int32, (x.shape[0], LANES), 1) < HEAD_DIM
    for c in range(x.shape[1] // LANES):
        a = x[:, c * LANES:(c + 1) * LANES]
        r = pltpu.roll(a, HEAD_DIM, 1)
        pieces += [jnp.where(lo, a, 0.0), jnp.where(lo, 0.0, r),
                   jnp.where(lo, r, 0.0), jnp.where(lo, 0.0, a)]
    return jnp.concatenate(pieces, axis=1)


def _swa_block(sink_ref, layer, not_first, q, kp_ref, kc, vp_ref, vc, fill):
    L = WINDOW
    qi = lax.broadcasted_iota(jnp.int32, (2 * L, 2 * L), 0) & (L - 1)
    kj = lax.broadcasted_iota(jnp.int32, (2 * L, 2 * L), 1)
    rel = qi + L - kj
    valid = (rel >= 0) & (rel < L) & ((kj >= L) | not_first)
    lower_rows = lax.broadcasted_iota(jnp.int32, (2 * L, 1), 0) >= L

    def softmax_parts(s, sink_top, sink_bot):
        s = jnp.where(valid, s, MASK_VALUE)
        sink = jnp.where(lower_rows, sink_bot, sink_top)
        m = jnp.maximum(jnp.max(s, axis=-1, keepdims=True), sink)
        e = jnp.exp(s - m)
        den = jnp.sum(e, axis=-1, keepdims=True) + jnp.exp(sink - m)
        return e.astype(BF16), 1.0 / den

    outs = []
    for hk in range(N_KV_HEADS):
        lo = slice(2 * LANES * hk, 2 * LANES * hk + LANES)
        hi = slice(2 * LANES * hk + LANES, 2 * LANES * (hk + 1))
        kz0 = jnp.concatenate([kp_ref[:, lo], kc[:, lo]], axis=0)
        kz1 = jnp.concatenate([kp_ref[:, hi], kc[:, hi]], axis=0)
        vz0 = jnp.concatenate([vp_ref[:, lo], vc[:, lo]], axis=0)
        vz1 = jnp.concatenate([vp_ref[:, hi], vc[:, hi]], axis=0)
        lhs = jnp.concatenate([q[:, lo], q[:, hi]], axis=0)
        sa = _dot_nt(lhs, kz0)
        sb = _dot_nt(lhs, kz1)
        fill(2)
        ea, ra = softmax_parts(sa, sink_ref[layer, 4 * hk], sink_ref[layer, 4 * hk + 2])
        eb, rb = softmax_parts(sb, sink_ref[layer, 4 * hk + 1], sink_ref[layer, 4 * hk + 3])
        comb = _dot(ea, vz0) * ra + _dot(eb, vz1) * rb
        outs += [comb[:L], comb[L:]]
    return jnp.concatenate(outs, axis=1)


class _HgrnConsts:
    def __init__(self):
        C = HGRN_CHUNK
        row = lax.broadcasted_iota(jnp.int32, (C, C), 0)
        col = lax.broadcasted_iota(jnp.int32, (C, C), 1)
        self.tri = (row >= col).astype(F32).astype(BF16)
        self.levels = []
        half = C // 2
        while half >= SUBLANES:
            shift = half.bit_length() - 1
            th = row >> shift
            sh = col >> shift
            self.levels.append((half, (th == sh + 1) & ((th & 1) == 1)))
            half //= 2
        self.sub = lax.broadcasted_iota(jnp.int32, (SUBLANES, LANES), 0)
        self.lane = lax.broadcasted_iota(jnp.int32, (SUBLANES, LANES), 1)


def _hgrn_head(cst, lf, q, k, v, st_ref, b_scr, k_scr, fill):
    C = HGRN_CHUNK
    g = lf * LOG2_E
    g1 = g.astype(BF16)
    r1 = g - g1.astype(F32)
    g2 = r1.astype(BF16)
    g3 = (r1 - g2.astype(F32)).astype(BF16)
    b = _dot(cst.tri, g1) + _dot(cst.tri, g2) + _dot(cst.tri, g3)
    b_scr[...] = b
    k_scr[...] = k

    a = jnp.zeros((C, C), F32)
    for half, mask in cst.levels:
        d = []
        for p0 in range(0, C, 2 * half):
            r = b[p0 + half - 1:p0 + half, :]
            d += [r - b[p0:p0 + half], b[p0 + half:p0 + 2 * half] - r]
        e = jnp.exp2(jnp.concatenate(d, axis=0))
        al = _dot_nt((q * e).astype(BF16), (k * e).astype(BF16))
        a = jnp.where(mask, al, a)

    fill(2)
    diag = []
    for r in range(C // SUBLANES):
        bb = b[r * SUBLANES:(r + 1) * SUBLANES]
        qb = q[r * SUBLANES:(r + 1) * SUBLANES]
        blk = jnp.zeros((SUBLANES, LANES), F32)
        for s in range(SUBLANES):
            t0 = r * SUBLANES + s
            bs = jnp.broadcast_to(b_scr[t0:t0 + 1, :], (SUBLANES, HGRN_DIM))
            ks = jnp.broadcast_to(k_scr[t0:t0 + 1, :], (SUBLANES, HGRN_DIM))
            w = jnp.sum(qb * ks * jnp.exp2(bb - bs), axis=-1, keepdims=True)
            blk = jnp.where((cst.lane == t0) & (cst.sub >= s), w, blk)
        diag.append(blk)
    a = a + jnp.concatenate(diag, axis=0)

    st = st_ref[...]
    o = _dot(a.astype(BF16), v) + _dot_nt((q * jnp.exp2(b)).astype(BF16), st.astype(BF16))
    b_last = b[C - 1:C, :]
    ke = (k * jnp.exp2(b_last - b)).astype(BF16)
    st_ref[...] = st * jnp.exp2(b_last) + _dot_tn(v, ke)
    return o


def _mixer_kernel(sink_ref, h0_ref, ha_ref, hb_ref, x_ref, w_ref, wo_ref,
                  cos0_ref, sin0_ref, cosa_ref, sina_ref, cosb_ref, sinb_ref, lb_ref,
                  ag_ref, hgain_ref, g1_ref, g2_ref, xo_ref, ho_ref,
                  kp_scr, vp_scr, *scr, layer):
    i = pl.program_id(1)
    C = HGRN_CHUNK
    nh = HGRN_HEADS
    stages = (scr[0:5], scr[5:10])
    st_refs = scr[10:10 + nh]
    b_scrs = scr[10 + nh:10 + 2 * nh]
    k_scrs = scr[10 + 2 * nh:10 + 3 * nh]
    st = SUB_TILE
    pw = 2 * HGRN_DIM

    def projection_units(h_ref, cos_ref, sin_ref, stage):
        q_s, k_s, v_s, act_s, lf_s = stage

        def proj(c0):
            return _dot(h_ref[...], w_ref[:, c0:c0 + st])

        def q_unit(c):
            def run():
                q_s[:, c * st:(c + 1) * st] = (
                    _rope(proj(c * st), cos_ref[...], sin_ref[...]) * (HEAD_DIM ** -0.5)
                ).astype(BF16)
            return run

        def k_unit():
            k_s[...] = _expand_k(
                _rope(proj(ATTN_WIDTH), cos_ref[...], sin_ref[...])).astype(BF16)

        def v_unit():
            v_s[...] = _expand_v(proj(ATTN_WIDTH + KV_WIDTH)).astype(BF16)

        def gate_unit(p, t):
            cols = slice(p * pw, (p + 1) * pw)

            def run():
                z = proj(QKV_COLS + p * PAIR_COLS + t * pw)
                if t == 1:
                    lb = lb_ref[:, cols]
                    s = _sigmoid(z)
                    lf_s[:, cols] = jnp.log(lb + (1.0 - lb) * s)
                    act_s[1, :, cols] = ((1.0 - lb) * (1.0 - s)).astype(BF16)
                elif t == 2:
                    act_s[2, :, cols] = z.astype(BF16)
                else:
                    act_s[0 if t == 0 else 3, :, cols] = (z * _sigmoid(z)).astype(BF16)
            return run

        units = [q_unit(c) for c in range(ATTN_WIDTH // st)] + [k_unit, v_unit]
        units += [gate_unit(p, t) for p in range(nh // 2) for t in range(4)]
        return units

    def mix(stage, rows, not_first, units):
        units = list(units)

        def fill(n):
            for _ in range(min(n, len(units))):
                units.pop(0)()

        q_s, k_s, v_s, act_s, lf_s = stage
        kc = k_s[...]
        vc = v_s[...]
        o = _swa_block(sink_ref, layer, not_first, q_s[...], kp_scr, kc, vp_scr, vc, fill)
        kp_scr[...] = kc
        vp_scr[...] = vc
        acc = _dot(_rms(o, ag_ref[...]).astype(BF16), wo_ref[:ATTN_WIDTH, :])
        cst = _HgrnConsts()
        ys = []
        for head in range(nh):
            sl = slice(head * HGRN_DIM, (head + 1) * HGRN_DIM)
            oh = _hgrn_head(cst, lf_s[:, sl], act_s[0, :, sl].astype(F32),
                            act_s[1, :, sl].astype(F32), act_s[2, :, sl],
                            st_refs[head], b_scrs[head], k_scrs[head], fill)
            ys.append((_rms(oh, hgain_ref[:, sl]) * act_s[3, :, sl].astype(F32)).astype(BF16))
        fill(len(units))
        acc = acc + _dot(jnp.concatenate(ys, axis=1), wo_ref[ATTN_WIDTH:, :])
        x1 = x_ref[rows, :] + _rms(acc, g1_ref[...])
        xo_ref[rows, :] = x1
        ho_ref[rows, :] = _rms(x1, g2_ref[...]).astype(ho_ref.dtype)

    @pl.when(i == 0)
    def _():
        for r in st_refs:
            r[...] = jnp.zeros_like(r)
        kp_scr[...] = jnp.zeros_like(kp_scr)
        vp_scr[...] = jnp.zeros_like(vp_scr)
        for unit in projection_units(h0_ref, cos0_ref, sin0_ref, stages[0]):
            unit()

    mix(stages[0], slice(0, C), i > 0, projection_units(ha_ref, cosa_ref, sina_ref, stages[1]))
    mix(stages[1], slice(C, 2 * C), True,
        projection_units(hb_ref, cosb_ref, sinb_ref, stages[0]))


def _mixer(sinks, h, x, w_in, w_out, cos, sin, lb, attn_g, hgrn_g, g1, g2, layer, batch, seq):
    C = HGRN_CHUNK
    nc = seq // C
    assert nc % 2 == 0 and w_in.shape[2] == QKV_COLS + (HGRN_HEADS // 2) * PAIR_COLS
    shp = lambda a: a.reshape(batch, seq, a.shape[-1])
    chunk0 = lambda w_: pl.BlockSpec((None, C, w_), lambda b, i: (b, 0, 0))
    odd = lambda w_: pl.BlockSpec((None, C, w_), lambda b, i: (b, 2 * i + 1, 0))
    nxt = lambda w_: pl.BlockSpec((None, C, w_),
                                  lambda b, i: (b, jnp.minimum(2 * i + 2, nc - 1), 0))
    two = pl.BlockSpec((None, 2 * C, D_MODEL), lambda b, i: (b, i, 0))
    vec = lambda w_: pl.BlockSpec((None, 1, w_), lambda b, i: (layer, 0, 0))
    resident = lambda r, c_: pl.BlockSpec((None, r, c_), lambda b, i: (layer, 0, 0),
                                           pipeline_mode=pl.Buffered(1))
    h3, x3, cos3, sin3 = shp(h), shp(x), shp(cos), shp(sin)
    stage = [pltpu.VMEM((C, 1024), BF16), pltpu.VMEM((C, 1024), BF16),
             pltpu.VMEM((C, 1024), BF16), pltpu.VMEM((4, C, HGRN_WIDTH), BF16),
             pltpu.VMEM((C, HGRN_WIDTH), F32)]
    xo, ho = pl.pallas_call(
        lambda *refs: _mixer_kernel(*refs, layer=layer),
        out_shape=(jax.ShapeDtypeStruct((batch, seq, D_MODEL), F32),
                   jax.ShapeDtypeStruct((batch, seq, D_MODEL), BF16)),
        grid=(batch, nc // 2),
        in_specs=[pl.BlockSpec(memory_space=pltpu.SMEM),
                  chunk0(D_MODEL), odd(D_MODEL), nxt(D_MODEL), two,
                  resident(D_MODEL, w_in.shape[2]), resident(D_MODEL, D_MODEL),
                  chunk0(LANES), chunk0(LANES), odd(LANES), odd(LANES), nxt(LANES), nxt(LANES),
                  vec(HGRN_WIDTH), vec(ATTN_WIDTH), vec(HGRN_WIDTH),
                  vec(D_MODEL), vec(D_MODEL)],
        out_specs=(two, two),
        scratch_shapes=[pltpu.VMEM((C, 1024), BF16), pltpu.VMEM((C, 1024), BF16)]
                       + stage + stage
                       + [pltpu.VMEM((C, HGRN_DIM), F32) for _ in range(3 * HGRN_HEADS)],
        compiler_params=pltpu.CompilerParams(
            dimension_semantics=("parallel", "arbitrary"), vmem_limit_bytes=MIXER_VMEM_LIMIT),
        name="mixer",
    )(sinks, h3, h3, h3, x3, w_in, w_out, cos3, sin3, cos3, sin3, cos3, sin3,
      lb, attn_g, hgrn_g, g1, g2)
    n = batch * seq
    return xo.reshape(n, D_MODEL), ho.reshape(n, D_MODEL)


def _ffn_kernel(h_ref, wg_ref, wu_ref, wd_ref, x_ref, g1_ref, g2_ref,
                xo_ref, ho_ref, acc_ref):
    j = pl.program_id(1)
    last = pl.num_programs(1) - 1

    def partial(rows):
        h = h_ref[rows, :]
        gate = _dot(h, wg_ref[...])
        up = _dot(h, wu_ref[...])
        act = (gate * _sigmoid(gate) * up).astype(BF16)
        return _dot(act, wd_ref[...])

    every_row = slice(0, h_ref.shape[0])

    @pl.when(j == 0)
    def _():
        acc_ref[...] = partial(every_row)

    @pl.when((j > 0) & (j < last))
    def _():
        acc_ref[...] += partial(every_row)

    @pl.when(j == last)
    def _():
        for r0 in range(0, h_ref.shape[0], ROW_SUB):
            rows = slice(r0, r0 + ROW_SUB)
            x2 = x_ref[rows, :] + _rms(acc_ref[rows, :] + partial(rows), g1_ref[...])
            xo_ref[rows, :] = x2
            ho_ref[rows, :] = _rms(x2, g2_ref[...]).astype(ho_ref.dtype)


def _ffn(h, wg, wu, wd, x, g1, g2, layer):
    n = x.shape[0]
    tm = ROW_TILE
    tf = COL_TILE
    row = lambda w_: pl.BlockSpec((tm, w_), lambda i, j: (i, 0))
    vec = pl.BlockSpec((None, 1, D_MODEL), lambda i, j: (layer, 0, 0))
    return pl.pallas_call(
        _ffn_kernel,
        out_shape=(jax.ShapeDtypeStruct((n, D_MODEL), F32),
                   jax.ShapeDtypeStruct((n, D_MODEL), BF16)),
        grid=(n // tm, D_FF // tf),
        in_specs=[row(D_MODEL),
                  pl.BlockSpec((None, D_MODEL, tf), lambda i, j: (layer, 0, j)),
                  pl.BlockSpec((None, D_MODEL, tf), lambda i, j: (layer, 0, j)),
                  pl.BlockSpec((None, tf, D_MODEL), lambda i, j: (layer, j, 0)),
                  row(D_MODEL), vec, vec],
        out_specs=(row(D_MODEL), row(D_MODEL)),
        scratch_shapes=[pltpu.VMEM((tm, D_MODEL), F32)],
        compiler_params=_params("parallel", "arbitrary"),
        name="ffn",
    )(h, wg, wu, wd, x, g1, g2)


def _ple_kernel(h_ref, wg_ref, p_ref, wp_ref, x_ref, *rest):
    for r0 in range(0, h_ref.shape[0], ROW_SUB):
        rows = slice(r0, r0 + ROW_SUB)
        gate = _sigmoid(_dot(h_ref[rows, :], wg_ref[...]))
        proj = _dot(p_ref[rows, :].astype(BF16), wp_ref[...])
        x3 = x_ref[rows, :] + proj * gate
        if len(rest) == 1:
            rest[0][rows, :] = x3
        else:
            g_ref, xo_ref, ho_ref = rest
            xo_ref[rows, :] = x3
            ho_ref[rows, :] = _rms(x3, g_ref[...]).astype(ho_ref.dtype)


def _ple(h, wg, p, wp, x, gains, layer, last):
    n = x.shape[0]
    tm = ROW_TILE
    nb = n // tm
    row = lambda w_: pl.BlockSpec((tm, w_), lambda i: (i, 0))
    in_specs = [row(D_MODEL),
                pl.BlockSpec((None, D_MODEL, D_MODEL), lambda i: (layer, 0, 0)),
                pl.BlockSpec((None, tm, D_PLE), lambda i: (layer, i, 0)),
                pl.BlockSpec((None, D_PLE, D_MODEL), lambda i: (layer, 0, 0)),
                row(D_MODEL)]
    x_shape = jax.ShapeDtypeStruct((n, D_MODEL), F32)
    args = (h, wg, p, wp, x)
    if last:
        out_shape, out_specs = x_shape, row(D_MODEL)
    else:
        in_specs.append(pl.BlockSpec((None, 1, D_MODEL), lambda i: (layer + 1, 0, 0)))
        args += (gains,)
        out_shape = (x_shape, jax.ShapeDtypeStruct((n, D_MODEL), BF16))
        out_specs = (row(D_MODEL), row(D_MODEL))
    out = pl.pallas_call(
        _ple_kernel,
        out_shape=out_shape,
        grid=(nb,),
        in_specs=in_specs,
        out_specs=out_specs,
        compiler_params=_params("parallel"),
        name="ple",
    )(*args)
    return (out, None) if last else out


def _pair_interleave(w):
    lead = w.shape[:-1]
    half = HEAD_DIM // 2
    w = w.reshape(*lead, -1, 2, 2, half)
    return jnp.swapaxes(w, -2, -3).reshape(*lead, -1)


def kernel(x, p, positions, w_in, attn_sinks, hgrn_lb_logits, attn_out_gain, hgrn_out_gain,
           w_out, pre_mix_gain, post_mix_gain, pre_ffn_gain, post_ffn_gain,
           w_ffn_gate, w_ffn_up, w_ffn_down, ple_gain, w_ple_gate, w_ple_proj):
    batch, seq, d = x.shape
    n = batch * seq
    depth = w_in.shape[0]
    half = HEAD_DIM // 2

    lb_soft = jax.nn.softmax(hgrn_lb_logits.astype(F32), axis=0)
    lower_bounds = (jnp.cumsum(lb_soft, axis=0) - lb_soft[0:1]).reshape(depth, 1, HGRN_WIDTH)

    inv_freq = ROPE_THETA ** (-jnp.arange(half, dtype=F32) / half)
    inv = jnp.tile(inv_freq, LANES // half).reshape(1, LANES)
    sgn = jnp.concatenate([-jnp.ones((LANES // 2,), F32),
                           jnp.ones((LANES // 2,), F32)]).reshape(1, LANES)
    cos, sin = _rope_tables(positions.reshape(n, 1), inv, sgn)

    kv0 = ATTN_WIDTH
    rest0 = ATTN_WIDTH + 2 * KV_WIDTH
    pw = 2 * HGRN_DIM
    per_pair = w_in[:, :, rest0:].reshape(depth, d, 4, HGRN_HEADS // 2, pw)
    per_pair = jnp.swapaxes(per_pair, 2, 3).reshape(depth, d, 4 * HGRN_WIDTH)
    w_in_b = jnp.concatenate(
        [_pair_interleave(w_in[:, :, :kv0]),
         _pair_interleave(w_in[:, :, kv0:kv0 + KV_WIDTH]),
         w_in[:, :, kv0 + KV_WIDTH:rest0], per_pair], axis=2).astype(BF16)
    w_out_b = w_out.astype(BF16)
    w_gate_b = w_ffn_gate.astype(BF16)
    w_up_b = w_ffn_up.astype(BF16)
    w_down_b = w_ffn_down.astype(BF16)
    w_pg_b = w_ple_gate.astype(BF16)
    w_pp_b = w_ple_proj.astype(BF16)
    p2 = p.reshape(depth, n, D_PLE)

    vecs = lambda a: a.astype(F32).reshape(depth, 1, -1)
    pre_mix, post_mix = vecs(pre_mix_gain), vecs(post_mix_gain)
    pre_ffn, post_ffn = vecs(pre_ffn_gain), vecs(post_ffn_gain)
    ple_g, attn_g, hgrn_g = vecs(ple_gain), vecs(attn_out_gain), vecs(hgrn_out_gain)
    sinks = attn_sinks.astype(F32)

    xf = x.reshape(n, d).astype(F32)
    h = _norm(xf, pre_mix, 0)
    for i in range(depth):
        xf, h = _mixer(sinks, h, xf, w_in_b, w_out_b, cos, sin, lower_bounds, attn_g, hgrn_g,
                       post_mix, pre_ffn, i, batch, seq)
        xf, h = _ffn(h, w_gate_b, w_up_b, w_down_b, xf, post_ffn, ple_g, i)
        xf, h = _ple(h, w_pg_b, p2, w_pp_b, xf, pre_mix, i, last=(i + 1 == depth))
    return xf.reshape(batch, seq, d).astype(x.dtype)
```

```python
import jax
import jax.numpy as jnp
from jax import lax
from jax.experimental import pallas as pl
from jax.experimental.pallas import tpu as pltpu

F32 = jnp.float32
BF16 = jnp.bfloat16

D_MODEL = 2048
ATTN_WIDTH = 1024
HGRN_WIDTH = 1024
HEAD_DIM = 64
N_Q_HEADS = 16
N_KV_HEADS = 4
WINDOW = 128
ROPE_THETA = 10000.0
MASK_VALUE = -1e30
HGRN_HEADS = 8
HGRN_DIM = 128
D_FF = 5632
D_PLE = 256
RMS_EPS = 1e-6
KV_WIDTH = N_KV_HEADS * HEAD_DIM
LOG2_E = 1.4426950408889634

LANES = 128
SUBLANES = 8
VMEM_LIMIT = 56 * 1024 * 1024
MIXER_VMEM_LIMIT = 61 * 1024 * 1024

ROW_TILE = 512
ROW_SUB = 256
COL_TILE = 512
SUB_TILE = 256
HGRN_CHUNK = 128


def _params(*sem):
    return pltpu.CompilerParams(dimension_semantics=sem, vmem_limit_bytes=VMEM_LIMIT)


def _rms(x, gain):
    ms = jnp.mean(x * x, axis=-1, keepdims=True)
    return x * lax.rsqrt(ms + RMS_EPS) * gain


def _sigmoid(x):
    return 1.0 / (1.0 + jnp.exp(-x))


def _dot(a, b):
    return jnp.dot(a, b, preferred_element_type=F32)


def _dot_nt(a, b):
    return lax.dot_general(a, b, (((1,), (1,)), ((), ())), preferred_element_type=F32)


def _dot_tn(a, b):
    return lax.dot_general(a, b, (((0,), (0,)), ((), ())), preferred_element_type=F32)


def _norm_kernel(x_ref, g_ref, o_ref):
    o_ref[...] = _rms(x_ref[...], g_ref[...]).astype(o_ref.dtype)


def _norm(x, gain, layer):
    n, d = x.shape
    return pl.pallas_call(
        _norm_kernel,
        out_shape=jax.ShapeDtypeStruct((n, d), BF16),
        grid=(n // ROW_TILE,),
        in_specs=[pl.BlockSpec((ROW_TILE, d), lambda i: (i, 0)),
                  pl.BlockSpec((None, 1, d), lambda i: (layer, 0, 0))],
        out_specs=pl.BlockSpec((ROW_TILE, d), lambda i: (i, 0)),
        compiler_params=_params("parallel"),
        name="pre_norm",
    )(x, gain)


def _rope_table_kernel(pos_ref, inv_ref, sgn_ref, cos_ref, sin_ref):
    ang = pos_ref[...].astype(F32) * inv_ref[...]
    cos_ref[...] = jnp.cos(ang)
    sin_ref[...] = jnp.sin(ang) * sgn_ref[...]


def _rope_tables(pos, inv, sgn):
    n = pos.shape[0]
    tm = 1024
    return pl.pallas_call(
        _rope_table_kernel,
        out_shape=(jax.ShapeDtypeStruct((n, LANES), F32),
                   jax.ShapeDtypeStruct((n, LANES), F32)),
        grid=(n // tm,),
        in_specs=[pl.BlockSpec((tm, 1), lambda i: (i, 0)),
                  pl.BlockSpec((1, LANES), lambda i: (0, 0)),
                  pl.BlockSpec((1, LANES), lambda i: (0, 0))],
        out_specs=(pl.BlockSpec((tm, LANES), lambda i: (i, 0)),
                   pl.BlockSpec((tm, LANES), lambda i: (i, 0))),
        compiler_params=_params("parallel"),
        name="rope_tables",
    )(pos, inv, sgn)


QKV_COLS = ATTN_WIDTH + 2 * KV_WIDTH


def _rope(x, cos, sin):
    half = HEAD_DIM // 2
    first_half = (lax.broadcasted_iota(jnp.int32, (x.shape[0], LANES), 1) & half) == 0
    parts = []
    for c in range(x.shape[1] // LANES):
        xc = x[:, c * LANES:(c + 1) * LANES]
        partner = jnp.where(first_half, pltpu.roll(xc, LANES - half, 1), pltpu.roll(xc, half, 1))
        parts.append(xc * cos + partner * sin)
    return jnp.concatenate(parts, axis=1)


def _expand_heads(x):
    pieces = []
    lo = lax.broadcasted_iota(jnp.int32, (x.shape[0], LANES), 1) < HEAD_DIM
    for c in range(x.shape[1] // LANES):
        a = x[:, c * LANES:(c + 1) * LANES]
        r = pltpu.roll(a, HEAD_DIM, 1)
        pieces += [jnp.where(lo, a, 0.0), jnp.where(lo, 0.0, r),
                   jnp.where(lo, r, 0.0), jnp.where(lo, 0.0, a)]
    return jnp.concatenate(pieces, axis=1)


def _swa_block(sink_ref, layer, not_first, q, kp_ref, kc, vp_ref, vc, fill):
    L = WINDOW
    qi = lax.broadcasted_iota(jnp.int32, (2 * L, 2 * L), 0) & (L - 1)
    kj = lax.broadcasted_iota(jnp.int32, (2 * L, 2 * L), 1)
    rel = qi + L - kj
    valid = (rel >= 0) & (rel < L) & ((kj >= L) | not_first)
    lower_rows = lax.broadcasted_iota(jnp.int32, (2 * L, 1), 0) >= L

    def softmax_parts(s, sink_top, sink_bot):
        s = jnp.where(valid, s, MASK_VALUE)
        sink = jnp.where(lower_rows, sink_bot, sink_top)
        m = jnp.maximum(jnp.max(s, axis=-1, keepdims=True), sink)
        e = jnp.exp(s - m)
        den = jnp.sum(e, axis=-1, keepdims=True) + jnp.exp(sink - m)
        return e.astype(BF16), 1.0 / den

    outs = []
    for hk in range(N_KV_HEADS):
        lo = slice(2 * LANES * hk, 2 * LANES * hk + LANES)
        hi = slice(2 * LANES * hk + LANES, 2 * LANES * (hk + 1))
        kz0 = jnp.concatenate([kp_ref[:, lo], kc[:, lo]], axis=0)
        kz1 = jnp.concatenate([kp_ref[:, hi], kc[:, hi]], axis=0)
        vz0 = jnp.concatenate([vp_ref[:, lo], vc[:, lo]], axis=0)
        vz1 = jnp.concatenate([vp_ref[:, hi], vc[:, hi]], axis=0)
        lhs = jnp.concatenate([q[:, lo], q[:, hi]], axis=0)
        sa = _dot_nt(lhs, kz0)
        sb = _dot_nt(lhs, kz1)
        fill(2)
        ea, ra = softmax_parts(sa, sink_ref[layer, 4 * hk], sink_ref[layer, 4 * hk + 2])
        eb, rb = softmax_parts(sb, sink_ref[layer, 4 * hk + 1], sink_ref[layer, 4 * hk + 3])
        comb = _dot(ea, vz0) * ra + _dot(eb, vz1) * rb
        outs += [comb[:L], comb[L:]]
    return jnp.concatenate(outs, axis=1)


class _HgrnConsts:
    def __init__(self):
        C = HGRN_CHUNK
        row = lax.broadcasted_iota(jnp.int32, (C, C), 0)
        col = lax.broadcasted_iota(jnp.int32, (C, C), 1)
        self.tri = (row >= col).astype(F32).astype(BF16)
        self.levels = []
        half = C // 2
        while half >= SUBLANES:
            shift = half.bit_length() - 1
            th = row >> shift
            sh = col >> shift
            self.levels.append((half, (th == sh + 1) & ((th & 1) == 1)))
            half //= 2
        self.sub = lax.broadcasted_iota(jnp.int32, (SUBLANES, LANES), 0)
        self.lane = lax.broadcasted_iota(jnp.int32, (SUBLANES, LANES), 1)


def _hgrn_head(cst, lf, q, k, v, st_ref, b_scr, k_scr, fill):
    C = HGRN_CHUNK
    g = lf * LOG2_E
    g1 = g.astype(BF16)
    r1 = g - g1.astype(F32)
    g2 = r1.astype(BF16)
    g3 = (r1 - g2.astype(F32)).astype(BF16)
    b = _dot(cst.tri, g1) + _dot(cst.tri, g2) + _dot(cst.tri, g3)
    b_scr[...] = b
    k_scr[...] = k
    fill(1)

    a = jnp.zeros((C, C), F32)
    for half, mask in cst.levels:
        d = []
        for p0 in range(0, C, 2 * half):
            r = b[p0 + half - 1:p0 + half, :]
            d += [r - b[p0:p0 + half], b[p0 + half:p0 + 2 * half] - r]
        e = jnp.exp2(jnp.concatenate(d, axis=0))
        al = _dot_nt((q * e).astype(BF16), (k * e).astype(BF16))
        a = jnp.where(mask, al, a)

    fill(1)
    diag = []
    for r in range(C // SUBLANES):
        bb = b[r * SUBLANES:(r + 1) * SUBLANES]
        qb = q[r * SUBLANES:(r + 1) * SUBLANES]
        blk = jnp.zeros((SUBLANES, LANES), F32)
        for s in range(SUBLANES):
            t0 = r * SUBLANES + s
            bs = jnp.broadcast_to(b_scr[t0:t0 + 1, :], (SUBLANES, HGRN_DIM))
            ks = jnp.broadcast_to(k_scr[t0:t0 + 1, :], (SUBLANES, HGRN_DIM))
            w = jnp.sum(qb * ks * jnp.exp2(bb - bs), axis=-1, keepdims=True)
            blk = jnp.where((cst.lane == t0) & (cst.sub >= s), w, blk)
        diag.append(blk)
    a = a + jnp.concatenate(diag, axis=0)

    st = st_ref[...]
    o = _dot(a.astype(BF16), v) + _dot_nt((q * jnp.exp2(b)).astype(BF16), st.astype(BF16))
    b_last = b[C - 1:C, :]
    ke = (k * jnp.exp2(b_last - b)).astype(BF16)
    st_ref[...] = st * jnp.exp2(b_last) + _dot_tn(v, ke)
    return o


def _mixer_kernel(sink_ref, h0_ref, ha_ref, hb_ref, x_ref, w_ref, wo_ref,
                  cos0_ref, sin0_ref, cosa_ref, sina_ref, cosb_ref, sinb_ref, lb_ref,
                  ag_ref, hgain_ref, g1_ref, g2_ref, xo_ref, ho_ref,
                  kp_scr, vp_scr, *scr, layer):
    i = pl.program_id(1)
    C = HGRN_CHUNK
    nh = HGRN_HEADS
    stages = (scr[0:5], scr[5:10])
    st_refs = scr[10:10 + nh]
    b_scrs = scr[10 + nh:10 + 2 * nh]
    k_scrs = scr[10 + 2 * nh:10 + 3 * nh]
    st = SUB_TILE
    pw = 2 * HGRN_DIM

    def projection_units(h_ref, cos_ref, sin_ref, stage):
        q_s, k_s, v_s, act_s, lf_s = stage

        def proj(c0):
            return _dot(h_ref[...], w_ref[:, c0:c0 + st])

        def q_unit(c):
            def run():
                q_s[:, c * st:(c + 1) * st] = (
                    _rope(proj(c * st), cos_ref[...], sin_ref[...]) * (HEAD_DIM ** -0.5)
                ).astype(BF16)
            return run

        def k_unit():
            k_s[...] = _expand_heads(
                _rope(proj(ATTN_WIDTH), cos_ref[...], sin_ref[...])).astype(BF16)

        def v_unit():
            v_s[...] = _expand_heads(proj(ATTN_WIDTH + KV_WIDTH)).astype(BF16)

        def gate_unit(p, t):
            cols = slice(p * pw, (p + 1) * pw)

            def run():
                z = proj(QKV_COLS + t * HGRN_WIDTH + p * pw)
                if t == 1:
                    lb = lb_ref[:, cols]
                    s = _sigmoid(z)
                    lf_s[:, cols] = jnp.log(lb + (1.0 - lb) * s)
                    act_s[1, :, cols] = ((1.0 - lb) * (1.0 - s)).astype(BF16)
                elif t == 2:
                    act_s[2, :, cols] = z.astype(BF16)
                else:
                    act_s[0 if t == 0 else 3, :, cols] = (z * _sigmoid(z)).astype(BF16)
            return run

        units = [q_unit(c) for c in range(ATTN_WIDTH // st)] + [k_unit, v_unit]
        units += [gate_unit(p, t) for p in range(nh // 2) for t in range(4)]
        return units

    def mix(stage, rows, not_first, units):
        units = list(units)

        def fill(n):
            for _ in range(min(n, len(units))):
                units.pop(0)()

        q_s, k_s, v_s, act_s, lf_s = stage
        kc = k_s[...]
        vc = v_s[...]
        o = _swa_block(sink_ref, layer, not_first, q_s[...], kp_scr, kc, vp_scr, vc, fill)
        kp_scr[...] = kc
        vp_scr[...] = vc
        acc = _dot(_rms(o, ag_ref[...]).astype(BF16), wo_ref[:ATTN_WIDTH, :])
        cst = _HgrnConsts()
        ys = []
        for head in range(nh):
            sl = slice(head * HGRN_DIM, (head + 1) * HGRN_DIM)
            oh = _hgrn_head(cst, lf_s[:, sl], act_s[0, :, sl].astype(F32),
                            act_s[1, :, sl].astype(F32), act_s[2, :, sl],
                            st_refs[head], b_scrs[head], k_scrs[head], fill)
            ys.append((_rms(oh, hgain_ref[:, sl]) * act_s[3, :, sl].astype(F32)).astype(BF16))
        fill(len(units))
        acc = acc + _dot(jnp.concatenate(ys, axis=1), wo_ref[ATTN_WIDTH:, :])
        x1 = x_ref[rows, :] + _rms(acc, g1_ref[...])
        xo_ref[rows, :] = x1
        ho_ref[rows, :] = _rms(x1, g2_ref[...]).astype(ho_ref.dtype)

    @pl.when(i == 0)
    def _():
        for r in st_refs:
            r[...] = jnp.zeros_like(r)
        kp_scr[...] = jnp.zeros_like(kp_scr)
        vp_scr[...] = jnp.zeros_like(vp_scr)
        for unit in projection_units(h0_ref, cos0_ref, sin0_ref, stages[0]):
            unit()

    mix(stages[0], slice(0, C), i > 0, projection_units(ha_ref, cosa_ref, sina_ref, stages[1]))
    mix(stages[1], slice(C, 2 * C), True,
        projection_units(hb_ref, cosb_ref, sinb_ref, stages[0]))


def _mixer(sinks, h, x, w_in, w_out, cos, sin, lb, attn_g, hgrn_g, g1, g2, layer, batch, seq):
    C = HGRN_CHUNK
    nc = seq // C
    assert nc % 2 == 0 and w_in.shape[2] == QKV_COLS + 4 * HGRN_WIDTH
    shp = lambda a: a.reshape(batch, seq, a.shape[-1])
    chunk0 = lambda w_: pl.BlockSpec((None, C, w_), lambda b, i: (b, 0, 0))
    odd = lambda w_: pl.BlockSpec((None, C, w_), lambda b, i: (b, 2 * i + 1, 0))
    nxt = lambda w_: pl.BlockSpec((None, C, w_),
                                  lambda b, i: (b, jnp.minimum(2 * i + 2, nc - 1), 0))
    two = pl.BlockSpec((None, 2 * C, D_MODEL), lambda b, i: (b, i, 0))
    vec = lambda w_: pl.BlockSpec((None, 1, w_), lambda b, i: (layer, 0, 0))
    resident = lambda r, c_: pl.BlockSpec((None, r, c_), lambda b, i: (layer, 0, 0),
                                           pipeline_mode=pl.Buffered(1))
    h3, x3, cos3, sin3 = shp(h), shp(x), shp(cos), shp(sin)
    stage = [pltpu.VMEM((C, 1024), BF16), pltpu.VMEM((C, 1024), BF16),
             pltpu.VMEM((C, 1024), BF16), pltpu.VMEM((4, C, HGRN_WIDTH), BF16),
             pltpu.VMEM((C, HGRN_WIDTH), F32)]
    xo, ho = pl.pallas_call(
        lambda *refs: _mixer_kernel(*refs, layer=layer),
        out_shape=(jax.ShapeDtypeStruct((batch, seq, D_MODEL), F32),
                   jax.ShapeDtypeStruct((batch, seq, D_MODEL), BF16)),
        grid=(batch, nc // 2),
        in_specs=[pl.BlockSpec(memory_space=pltpu.SMEM),
                  chunk0(D_MODEL), odd(D_MODEL), nxt(D_MODEL), two,
                  resident(D_MODEL, w_in.shape[2]), resident(D_MODEL, D_MODEL),
                  chunk0(LANES), chunk0(LANES), odd(LANES), odd(LANES), nxt(LANES), nxt(LANES),
                  vec(HGRN_WIDTH), vec(ATTN_WIDTH), vec(HGRN_WIDTH),
                  vec(D_MODEL), vec(D_MODEL)],
        out_specs=(two, two),
        scratch_shapes=[pltpu.VMEM((C, 1024), BF16), pltpu.VMEM((C, 1024), BF16)]
                       + stage + stage
                       + [pltpu.VMEM((C, HGRN_DIM), F32) for _ in range(3 * HGRN_HEADS)],
        compiler_params=pltpu.CompilerParams(
            dimension_semantics=("parallel", "arbitrary"), vmem_limit_bytes=MIXER_VMEM_LIMIT),
        name="mixer",
    )(sinks, h3, h3, h3, x3, w_in, w_out, cos3, sin3, cos3, sin3, cos3, sin3,
      lb, attn_g, hgrn_g, g1, g2)
    n = batch * seq
    return xo.reshape(n, D_MODEL), ho.reshape(n, D_MODEL)


def _ffn_kernel(h_ref, wg_ref, wu_ref, wd_ref, x_ref, g1_ref, g2_ref,
                xo_ref, ho_ref, acc_ref):
    j = pl.program_id(1)
    last = pl.num_programs(1) - 1

    def partial(rows):
        h = h_ref[rows, :]
        gate = _dot(h, wg_ref[...])
        up = _dot(h, wu_ref[...])
        act = (gate * _sigmoid(gate) * up).astype(BF16)
        return _dot(act, wd_ref[...])

    every_row = slice(0, h_ref.shape[0])

    @pl.when(j == 0)
    def _():
        acc_ref[...] = partial(every_row)

    @pl.when((j > 0) & (j < last))
    def _():
        acc_ref[...] += partial(every_row)

    @pl.when(j == last)
    def _():
        for r0 in range(0, h_ref.shape[0], ROW_SUB):
            rows = slice(r0, r0 + ROW_SUB)
            x2 = x_ref[rows, :] + _rms(acc_ref[rows, :] + partial(rows), g1_ref[...])
            xo_ref[rows, :] = x2
            ho_ref[rows, :] = _rms(x2, g2_ref[...]).astype(ho_ref.dtype)


def _ffn(h, wg, wu, wd, x, g1, g2, layer):
    n = x.shape[0]
    tm = ROW_TILE
    tf = COL_TILE
    row = lambda w_: pl.BlockSpec((tm, w_), lambda i, j: (i, 0))
    vec = pl.BlockSpec((None, 1, D_MODEL), lambda i, j: (layer, 0, 0))
    return pl.pallas_call(
        _ffn_kernel,
        out_shape=(jax.ShapeDtypeStruct((n, D_MODEL), F32),
                   jax.ShapeDtypeStruct((n, D_MODEL), BF16)),
        grid=(n // tm, D_FF // tf),
        in_specs=[row(D_MODEL),
                  pl.BlockSpec((None, D_MODEL, tf), lambda i, j: (layer, 0, j)),
                  pl.BlockSpec((None, D_MODEL, tf), lambda i, j: (layer, 0, j)),
                  pl.BlockSpec((None, tf, D_MODEL), lambda i, j: (layer, j, 0)),
                  row(D_MODEL), vec, vec],
        out_specs=(row(D_MODEL), row(D_MODEL)),
        scratch_shapes=[pltpu.VMEM((tm, D_MODEL), F32)],
        compiler_params=_params("parallel", "arbitrary"),
        name="ffn",
    )(h, wg, wu, wd, x, g1, g2)


def _ple_kernel(h_ref, wg_ref, p_ref, wp_ref, x_ref, *rest):
    for r0 in range(0, h_ref.shape[0], ROW_SUB):
        rows = slice(r0, r0 + ROW_SUB)
        gate = _sigmoid(_dot(h_ref[rows, :], wg_ref[...]))
        proj = _dot(p_ref[rows, :].astype(BF16), wp_ref[...])
        x3 = x_ref[rows, :] + proj * gate
        if len(rest) == 1:
            rest[0][rows, :] = x3
        else:
            g_ref, xo_ref, ho_ref = rest
            xo_ref[rows, :] = x3
            ho_ref[rows, :] = _rms(x3, g_ref[...]).astype(ho_ref.dtype)


def _ple(h, wg, p, wp, x, gains, layer, last):
    n = x.shape[0]
    tm = ROW_TILE
    nb = n // tm
    row = lambda w_: pl.BlockSpec((tm, w_), lambda i: (i, 0))
    in_specs = [row(D_MODEL),
                pl.BlockSpec((None, D_MODEL, D_MODEL), lambda i: (layer, 0, 0)),
                pl.BlockSpec((None, tm, D_PLE), lambda i: (layer, i, 0)),
                pl.BlockSpec((None, D_PLE, D_MODEL), lambda i: (layer, 0, 0)),
                row(D_MODEL)]
    x_shape = jax.ShapeDtypeStruct((n, D_MODEL), F32)
    args = (h, wg, p, wp, x)
    if last:
        out_shape, out_specs = x_shape, row(D_MODEL)
    else:
        in_specs.append(pl.BlockSpec((None, 1, D_MODEL), lambda i: (layer + 1, 0, 0)))
        args += (gains,)
        out_shape = (x_shape, jax.ShapeDtypeStruct((n, D_MODEL), BF16))
        out_specs = (row(D_MODEL), row(D_MODEL))
    out = pl.pallas_call(
        _ple_kernel,
        out_shape=out_shape,
        grid=(nb,),
        in_specs=in_specs,
        out_specs=out_specs,
        compiler_params=_params("parallel"),
        name="ple",
    )(*args)
    return (out, None) if last else out


def kernel(x, p, positions, w_in, attn_sinks, hgrn_lb_logits, attn_out_gain, hgrn_out_gain,
           w_out, pre_mix_gain, post_mix_gain, pre_ffn_gain, post_ffn_gain,
           w_ffn_gate, w_ffn_up, w_ffn_down, ple_gain, w_ple_gate, w_ple_proj):
    batch, seq, d = x.shape
    n = batch * seq
    depth = w_in.shape[0]
    half = HEAD_DIM // 2

    lb_soft = jax.nn.softmax(hgrn_lb_logits.astype(F32), axis=0)
    lower_bounds = (jnp.cumsum(lb_soft, axis=0) - lb_soft[0:1]).reshape(depth, 1, HGRN_WIDTH)

    inv_freq = ROPE_THETA ** (-jnp.arange(half, dtype=F32) / half)
    inv = jnp.tile(inv_freq, LANES // half).reshape(1, LANES)
    sgn = jnp.tile(jnp.concatenate([-jnp.ones((half,), F32), jnp.ones((half,), F32)]),
                   LANES // HEAD_DIM).reshape(1, LANES)
    cos, sin = _rope_tables(positions.reshape(n, 1), inv, sgn)

    w_in_b = w_in.astype(BF16)
    w_out_b = w_out.astype(BF16)
    w_gate_b = w_ffn_gate.astype(BF16)
    w_up_b = w_ffn_up.astype(BF16)
    w_down_b = w_ffn_down.astype(BF16)
    w_pg_b = w_ple_gate.astype(BF16)
    w_pp_b = w_ple_proj.astype(BF16)
    p2 = p.reshape(depth, n, D_PLE)

    vecs = lambda a: a.astype(F32).reshape(depth, 1, -1)
    pre_mix, post_mix = vecs(pre_mix_gain), vecs(post_mix_gain)
    pre_ffn, post_ffn = vecs(pre_ffn_gain), vecs(post_ffn_gain)
    ple_g, attn_g, hgrn_g = vecs(ple_gain), vecs(attn_out_gain), vecs(hgrn_out_gain)
    sinks = attn_sinks.astype(F32)

    xf = x.reshape(n, d).astype(F32)
    h = _norm(xf, pre_mix, 0)
    for i in range(depth):
        xf, h = _mixer(sinks, h, xf, w_in_b, w_out_b, cos, sin, lower_bounds, attn_g, hgrn_g,
                       post_mix, pre_ffn, i, batch, seq)
        xf, h = _ffn(h, w_gate_b, w_up_b, w_down_b, xf, post_ffn, ple_g, i)
        xf, h = _ple(h, w_pg_b, p2, w_pp_b, xf, pre_mix, i, last=(i + 1 == depth))
    return xf.reshape(batch, seq, d).astype(x.dtype)
```

```python
import jax
import jax.numpy as jnp
from jax import lax
from jax.experimental import pallas as pl
from jax.experimental.pallas import tpu as pltpu

F32 = jnp.float32
BF16 = jnp.bfloat16

D_MODEL = 2048
ATTN_WIDTH = 1024
HGRN_WIDTH = 1024
HEAD_DIM = 64
N_Q_HEADS = 16
N_KV_HEADS = 4
WINDOW = 128
ROPE_THETA = 10000.0
MASK_VALUE = -1e30
HGRN_HEADS = 8
HGRN_DIM = 128
D_FF = 5632
D_PLE = 256
RMS_EPS = 1e-6
KV_WIDTH = N_KV_HEADS * HEAD_DIM
LOG2_E = 1.4426950408889634

LANES = 128
SUBLANES = 8
VMEM_LIMIT = 56 * 1024 * 1024
MIXER_VMEM_LIMIT = 61 * 1024 * 1024

ROW_TILE = 512
ROW_SUB = 256
COL_TILE = 512
SUB_TILE = 256
HGRN_CHUNK = 128


def _params(*sem):
    return pltpu.CompilerParams(dimension_semantics=sem, vmem_limit_bytes=VMEM_LIMIT)


def _rms(x, gain):
    ms = jnp.mean(x * x, axis=-1, keepdims=True)
    return x * lax.rsqrt(ms + RMS_EPS) * gain


def _sigmoid(x):
    return 1.0 / (1.0 + jnp.exp(-x))


def _dot(a, b):
    return jnp.dot(a, b, preferred_element_type=F32)


def _dot_nt(a, b):
    return lax.dot_general(a, b, (((1,), (1,)), ((), ())), preferred_element_type=F32)


def _dot_tn(a, b):
    return lax.dot_general(a, b, (((0,), (0,)), ((), ())), preferred_element_type=F32)


def _norm_kernel(x_ref, g_ref, o_ref):
    o_ref[...] = _rms(x_ref[...], g_ref[...]).astype(o_ref.dtype)


def _norm(x, gain, layer):
    n, d = x.shape
    return pl.pallas_call(
        _norm_kernel,
        out_shape=jax.ShapeDtypeStruct((n, d), BF16),
        grid=(n // ROW_TILE,),
        in_specs=[pl.BlockSpec((ROW_TILE, d), lambda i: (i, 0)),
                  pl.BlockSpec((None, 1, d), lambda i: (layer, 0, 0))],
        out_specs=pl.BlockSpec((ROW_TILE, d), lambda i: (i, 0)),
        compiler_params=_params("parallel"),
        name="pre_norm",
    )(x, gain)


def _rope_table_kernel(pos_ref, inv_ref, sgn_ref, cos_ref, sin_ref):
    ang = pos_ref[...].astype(F32) * inv_ref[...]
    cos_ref[...] = jnp.cos(ang)
    sin_ref[...] = jnp.sin(ang) * sgn_ref[...]


def _rope_tables(pos, inv, sgn):
    n = pos.shape[0]
    tm = 1024
    return pl.pallas_call(
        _rope_table_kernel,
        out_shape=(jax.ShapeDtypeStruct((n, LANES), F32),
                   jax.ShapeDtypeStruct((n, LANES), F32)),
        grid=(n // tm,),
        in_specs=[pl.BlockSpec((tm, 1), lambda i: (i, 0)),
                  pl.BlockSpec((1, LANES), lambda i: (0, 0)),
                  pl.BlockSpec((1, LANES), lambda i: (0, 0))],
        out_specs=(pl.BlockSpec((tm, LANES), lambda i: (i, 0)),
                   pl.BlockSpec((tm, LANES), lambda i: (i, 0))),
        compiler_params=_params("parallel"),
        name="rope_tables",
    )(pos, inv, sgn)


QKV_COLS = ATTN_WIDTH + 2 * KV_WIDTH


def _rope(x, cos, sin):
    half = HEAD_DIM // 2
    first_half = (lax.broadcasted_iota(jnp.int32, (x.shape[0], LANES), 1) & half) == 0
    parts = []
    for c in range(x.shape[1] // LANES):
        xc = x[:, c * LANES:(c + 1) * LANES]
        partner = jnp.where(first_half, pltpu.roll(xc, LANES - half, 1), pltpu.roll(xc, half, 1))
        parts.append(xc * cos + partner * sin)
    return jnp.concatenate(parts, axis=1)


def _expand_heads(x):
    pieces = []
    lo = lax.broadcasted_iota(jnp.int32, (x.shape[0], LANES), 1) < HEAD_DIM
    for c in range(x.shape[1] // LANES):
        a = x[:, c * LANES:(c + 1) * LANES]
        r = pltpu.roll(a, HEAD_DIM, 1)
        pieces += [jnp.where(lo, a, 0.0), jnp.where(lo, 0.0, r),
                   jnp.where(lo, r, 0.0), jnp.where(lo, 0.0, a)]
    return jnp.concatenate(pieces, axis=1)


def _swa_block(sink_ref, layer, not_first, q, kp_ref, kc, vp_ref, vc, fill):
    L = WINDOW
    qi = lax.broadcasted_iota(jnp.int32, (2 * L, 2 * L), 0) & (L - 1)
    kj = lax.broadcasted_iota(jnp.int32, (2 * L, 2 * L), 1)
    rel = qi + L - kj
    valid = (rel >= 0) & (rel < L) & ((kj >= L) | not_first)
    lower_rows = lax.broadcasted_iota(jnp.int32, (2 * L, 1), 0) >= L

    def softmax_parts(s, sink_top, sink_bot):
        s = jnp.where(valid, s, MASK_VALUE)
        sink = jnp.where(lower_rows, sink_bot, sink_top)
        m = jnp.maximum(jnp.max(s, axis=-1, keepdims=True), sink)
        e = jnp.exp(s - m)
        den = jnp.sum(e, axis=-1, keepdims=True) + jnp.exp(sink - m)
        return e.astype(BF16), 1.0 / den

    outs = []
    for hk in range(N_KV_HEADS):
        lo = slice(2 * LANES * hk, 2 * LANES * hk + LANES)
        hi = slice(2 * LANES * hk + LANES, 2 * LANES * (hk + 1))
        kz0 = jnp.concatenate([kp_ref[:, lo], kc[:, lo]], axis=0)
        kz1 = jnp.concatenate([kp_ref[:, hi], kc[:, hi]], axis=0)
        vz0 = jnp.concatenate([vp_ref[:, lo], vc[:, lo]], axis=0)
        vz1 = jnp.concatenate([vp_ref[:, hi], vc[:, hi]], axis=0)
        lhs = jnp.concatenate([q[:, lo], q[:, hi]], axis=0)
        sa = _dot_nt(lhs, kz0)
        sb = _dot_nt(lhs, kz1)
        fill(2)
        ea, ra = softmax_parts(sa, sink_ref[layer, 4 * hk], sink_ref[layer, 4 * hk + 2])
        eb, rb = softmax_parts(sb, sink_ref[layer, 4 * hk + 1], sink_ref[layer, 4 * hk + 3])
        comb = _dot(ea, vz0) * ra + _dot(eb, vz1) * rb
        outs += [comb[:L], comb[L:]]
    return jnp.concatenate(outs, axis=1)


class _HgrnConsts:
    def __init__(self):
        C = HGRN_CHUNK
        row = lax.broadcasted_iota(jnp.int32, (C, C), 0)
        col = lax.broadcasted_iota(jnp.int32, (C, C), 1)
        self.tri = (row >= col).astype(F32).astype(BF16)
        self.levels = []
        half = C // 2
        while half >= SUBLANES:
            shift = half.bit_length() - 1
            th = row >> shift
            sh = col >> shift
            self.levels.append((half, (th == sh + 1) & ((th & 1) == 1)))
            half //= 2
        self.sub = lax.broadcasted_iota(jnp.int32, (SUBLANES, LANES), 0)
        self.lane = lax.broadcasted_iota(jnp.int32, (SUBLANES, LANES), 1)


def _hgrn_head(cst, lf, q, k, v, st_ref, b_scr, k_scr, fill):
    C = HGRN_CHUNK
    g = lf * LOG2_E
    g1 = g.astype(BF16)
    r1 = g - g1.astype(F32)
    g2 = r1.astype(BF16)
    g3 = (r1 - g2.astype(F32)).astype(BF16)
    b = _dot(cst.tri, g1) + _dot(cst.tri, g2) + _dot(cst.tri, g3)
    b_scr[...] = b
    k_scr[...] = k
    fill(1)

    a = jnp.zeros((C, C), F32)
    for half, mask in cst.levels:
        d = []
        for p0 in range(0, C, 2 * half):
            r = b[p0 + half - 1:p0 + half, :]
            d += [r - b[p0:p0 + half], b[p0 + half:p0 + 2 * half] - r]
        e = jnp.exp2(jnp.concatenate(d, axis=0))
        al = _dot_nt((q * e).astype(BF16), (k * e).astype(BF16))
        a = jnp.where(mask, al, a)

    fill(1)
    diag = []
    for r in range(C // SUBLANES):
        bb = b[r * SUBLANES:(r + 1) * SUBLANES]
        qb = q[r * SUBLANES:(r + 1) * SUBLANES]
        blk = jnp.zeros((SUBLANES, LANES), F32)
        for s in range(SUBLANES):
            t0 = r * SUBLANES + s
            bs = jnp.broadcast_to(b_scr[t0:t0 + 1, :], (SUBLANES, HGRN_DIM))
            ks = jnp.broadcast_to(k_scr[t0:t0 + 1, :], (SUBLANES, HGRN_DIM))
            w = jnp.sum(qb * ks * jnp.exp2(bb - bs), axis=-1, keepdims=True)
            blk = jnp.where((cst.lane == t0) & (cst.sub >= s), w, blk)
        diag.append(blk)
    a = a + jnp.concatenate(diag, axis=0)

    st = st_ref[...]
    o = _dot(a.astype(BF16), v) + _dot_nt((q * jnp.exp2(b)).astype(BF16), st.astype(BF16))
    b_last = b[C - 1:C, :]
    ke = (k * jnp.exp2(b_last - b)).astype(BF16)
    st_ref[...] = st * jnp.exp2(b_last) + _dot_tn(v, ke)
    return o


def _mixer_kernel(sink_ref, h0_ref, ha_ref, hb_ref, x_ref, w_ref, wo_ref,
                  cos0_ref, sin0_ref, cosa_ref, sina_ref, cosb_ref, sinb_ref, lb_ref,
                  ag_ref, hgain_ref, g1_ref, g2_ref, xo_ref, ho_ref,
                  kp_scr, vp_scr, *scr, layer):
    i = pl.program_id(1)
    C = HGRN_CHUNK
    nh = HGRN_HEADS
    stages = (scr[0:5], scr[5:10])
    st_refs = scr[10:10 + nh]
    b_scrs = scr[10 + nh:10 + 2 * nh]
    k_scrs = scr[10 + 2 * nh:10 + 3 * nh]
    st = SUB_TILE
    pw = 2 * HGRN_DIM

    def projection_units(h_ref, cos_ref, sin_ref, stage):
        q_s, k_s, v_s, act_s, lf_s = stage

        def proj(c0):
            return _dot(h_ref[...], w_ref[:, c0:c0 + st])

        def q_unit(c):
            def run():
                q_s[:, c * st:(c + 1) * st] = (
                    _rope(proj(c * st), cos_ref[...], sin_ref[...]) * (HEAD_DIM ** -0.5)
                ).astype(BF16)
            return run

        def k_unit():
            k_s[...] = _expand_heads(
                _rope(proj(ATTN_WIDTH), cos_ref[...], sin_ref[...])).astype(BF16)

        def v_unit():
            v_s[...] = _expand_heads(proj(ATTN_WIDTH + KV_WIDTH)).astype(BF16)

        def gate_unit(p, t):
            cols = slice(p * pw, (p + 1) * pw)

            def run():
                z = proj(QKV_COLS + t * HGRN_WIDTH + p * pw)
                if t == 1:
                    lb = lb_ref[:, cols]
                    s = _sigmoid(z)
                    lf_s[:, cols] = jnp.log(lb + (1.0 - lb) * s)
                    act_s[1, :, cols] = ((1.0 - lb) * (1.0 - s)).astype(BF16)
                elif t == 2:
                    act_s[2, :, cols] = z.astype(BF16)
                else:
                    act_s[0 if t == 0 else 3, :, cols] = (z * _sigmoid(z)).astype(BF16)
            return run

        units = [q_unit(c) for c in range(ATTN_WIDTH // st)] + [k_unit, v_unit]
        units += [gate_unit(p, t) for p in range(nh // 2) for t in range(4)]
        return units

    def mix(stage, rows, not_first, units):
        units = list(units)

        def fill(n):
            for _ in range(min(n, len(units))):
                units.pop(0)()

        q_s, k_s, v_s, act_s, lf_s = stage
        kc = k_s[...]
        vc = v_s[...]
        o = _swa_block(sink_ref, layer, not_first, q_s[...], kp_scr, kc, vp_scr, vc, fill)
        kp_scr[...] = kc
        vp_scr[...] = vc
        acc = _dot(_rms(o, ag_ref[...]).astype(BF16), wo_ref[:ATTN_WIDTH, :])
        cst = _HgrnConsts()
        ys = []
        for head in range(nh):
            sl = slice(head * HGRN_DIM, (head + 1) * HGRN_DIM)
            oh = _hgrn_head(cst, lf_s[:, sl], act_s[0, :, sl].astype(F32),
                            act_s[1, :, sl].astype(F32), act_s[2, :, sl],
                            st_refs[head], b_scrs[head], k_scrs[head], fill)
            ys.append((_rms(oh, hgain_ref[:, sl]) * act_s[3, :, sl].astype(F32)).astype(BF16))
        fill(len(units))
        acc = acc + _dot(jnp.concatenate(ys, axis=1), wo_ref[ATTN_WIDTH:, :])
        x1 = x_ref[rows, :] + _rms(acc, g1_ref[...])
        xo_ref[rows, :] = x1
        ho_ref[rows, :] = _rms(x1, g2_ref[...]).astype(ho_ref.dtype)

    @pl.when(i == 0)
    def _():
        for r in st_refs:
            r[...] = jnp.zeros_like(r)
        kp_scr[...] = jnp.zeros_like(kp_scr)
        vp_scr[...] = jnp.zeros_like(vp_scr)
        for unit in projection_units(h0_ref, cos0_ref, sin0_ref, stages[0]):
            unit()

    mix(stages[0], slice(0, C), i > 0, projection_units(ha_ref, cosa_ref, sina_ref, stages[1]))
    mix(stages[1], slice(C, 2 * C), True,
        projection_units(hb_ref, cosb_ref, sinb_ref, stages[0]))


def _mixer(sinks, h, x, w_in, w_out, cos, sin, lb, attn_g, hgrn_g, g1, g2, layer, batch, seq):
    C = HGRN_CHUNK
    nc = seq // C
    assert nc % 2 == 0 and w_in.shape[2] == QKV_COLS + 4 * HGRN_WIDTH
    shp = lambda a: a.reshape(batch, seq, a.shape[-1])
    chunk0 = lambda w_: pl.BlockSpec((None, C, w_), lambda b, i: (b, 0, 0))
    odd = lambda w_: pl.BlockSpec((None, C, w_), lambda b, i: (b, 2 * i + 1, 0))
    nxt = lambda w_: pl.BlockSpec((None, C, w_),
                                  lambda b, i: (b, jnp.minimum(2 * i + 2, nc - 1), 0))
    two = pl.BlockSpec((None, 2 * C, D_MODEL), lambda b, i: (b, i, 0))
    vec = lambda w_: pl.BlockSpec((None, 1, w_), lambda b, i: (layer, 0, 0))
    resident = lambda r, c_: pl.BlockSpec((None, r, c_), lambda b, i: (layer, 0, 0),
                                           pipeline_mode=pl.Buffered(1))
    h3, x3, cos3, sin3 = shp(h), shp(x), shp(cos), shp(sin)
    stage = [pltpu.VMEM((C, 1024), BF16), pltpu.VMEM((C, 1024), BF16),
             pltpu.VMEM((C, 1024), BF16), pltpu.VMEM((4, C, HGRN_WIDTH), BF16),
             pltpu.VMEM((C, HGRN_WIDTH), F32)]
    xo, ho = pl.pallas_call(
        lambda *refs: _mixer_kernel(*refs, layer=layer),
        out_shape=(jax.ShapeDtypeStruct((batch, seq, D_MODEL), F32),
                   jax.ShapeDtypeStruct((batch, seq, D_MODEL), BF16)),
        grid=(batch, nc // 2),
        in_specs=[pl.BlockSpec(memory_space=pltpu.SMEM),
                  chunk0(D_MODEL), odd(D_MODEL), nxt(D_MODEL), two,
                  resident(D_MODEL, w_in.shape[2]), resident(D_MODEL, D_MODEL),
                  chunk0(LANES), chunk0(LANES), odd(LANES), odd(LANES), nxt(LANES), nxt(LANES),
                  vec(HGRN_WIDTH), vec(ATTN_WIDTH), vec(HGRN_WIDTH),
                  vec(D_MODEL), vec(D_MODEL)],
        out_specs=(two, two),
        scratch_shapes=[pltpu.VMEM((C, 1024), BF16), pltpu.VMEM((C, 1024), BF16)]
                       + stage + stage
                       + [pltpu.VMEM((C, HGRN_DIM), F32) for _ in range(3 * HGRN_HEADS)],
        compiler_params=pltpu.CompilerParams(
            dimension_semantics=("parallel", "arbitrary"), vmem_limit_bytes=MIXER_VMEM_LIMIT),
        name="mixer",
    )(sinks, h3, h3, h3, x3, w_in, w_out, cos3, sin3, cos3, sin3, cos3, sin3,
      lb, attn_g, hgrn_g, g1, g2)
    n = batch * seq
    return xo.reshape(n, D_MODEL), ho.reshape(n, D_MODEL)


def _ffn_kernel(h_ref, wg_ref, wu_ref, wd_ref, x_ref, g1_ref, g2_ref,
                wpg_ref, p_ref, wpp_ref, *rest):
    acc_ref = rest[-1]
    j = pl.program_id(1)
    last = pl.num_programs(1) - 1

    def partial(rows):
        h = h_ref[rows, :]
        gate = _dot(h, wg_ref[...])
        up = _dot(h, wu_ref[...])
        act = (gate * _sigmoid(gate) * up).astype(BF16)
        return _dot(act, wd_ref[...])

    every_row = slice(0, h_ref.shape[0])

    @pl.when(j == 0)
    def _():
        acc_ref[...] = partial(every_row)

    @pl.when((j > 0) & (j < last))
    def _():
        acc_ref[...] += partial(every_row)

    @pl.when(j == last)
    def _():
        for r0 in range(0, h_ref.shape[0], ROW_SUB):
            rows = slice(r0, r0 + ROW_SUB)
            x2 = x_ref[rows, :] + _rms(acc_ref[rows, :] + partial(rows), g1_ref[...])
            h3 = _rms(x2, g2_ref[...]).astype(BF16)
            gate = _sigmoid(_dot(h3, wpg_ref[...]))
            proj = _dot(p_ref[rows, :].astype(BF16), wpp_ref[...])
            x3 = x2 + proj * gate
            if len(rest) == 2:
                rest[0][rows, :] = x3
            else:
                g3_ref, xo_ref, ho_ref, _ = rest
                xo_ref[rows, :] = x3
                ho_ref[rows, :] = _rms(x3, g3_ref[...]).astype(ho_ref.dtype)


def _ffn(h, wg, wu, wd, x, g1, g2, wpg, p, wpp, g3, layer, last):
    n = x.shape[0]
    tm = ROW_TILE
    tf = COL_TILE
    row = lambda w_: pl.BlockSpec((tm, w_), lambda i, j: (i, 0))
    vec = pl.BlockSpec((None, 1, D_MODEL), lambda i, j: (layer, 0, 0))
    resident = lambda r: pl.BlockSpec((None, r, D_MODEL), lambda i, j: (layer, 0, 0),
                                      pipeline_mode=pl.Buffered(1))
    in_specs = [row(D_MODEL),
                pl.BlockSpec((None, D_MODEL, tf), lambda i, j: (layer, 0, j)),
                pl.BlockSpec((None, D_MODEL, tf), lambda i, j: (layer, 0, j)),
                pl.BlockSpec((None, tf, D_MODEL), lambda i, j: (layer, j, 0)),
                row(D_MODEL), vec, vec,
                resident(D_MODEL),
                pl.BlockSpec((None, tm, D_PLE), lambda i, j: (layer, i, 0)),
                resident(D_PLE)]
    args = (h, wg, wu, wd, x, g1, g2, wpg, p, wpp)
    x_shape = jax.ShapeDtypeStruct((n, D_MODEL), F32)
    if last:
        out_shape, out_specs = x_shape, row(D_MODEL)
    else:
        in_specs.append(pl.BlockSpec((None, 1, D_MODEL), lambda i, j: (layer + 1, 0, 0)))
        args += (g3,)
        out_shape = (x_shape, jax.ShapeDtypeStruct((n, D_MODEL), BF16))
        out_specs = (row(D_MODEL), row(D_MODEL))
    out = pl.pallas_call(
        _ffn_kernel,
        out_shape=out_shape,
        grid=(n // tm, D_FF // tf),
        in_specs=in_specs,
        out_specs=out_specs,
        scratch_shapes=[pltpu.VMEM((tm, D_MODEL), F32)],
        compiler_params=_params("parallel", "arbitrary"),
        name="ffn_ple",
    )(*args)
    return (out, None) if last else out


def kernel(x, p, positions, w_in, attn_sinks, hgrn_lb_logits, attn_out_gain, hgrn_out_gain,
           w_out, pre_mix_gain, post_mix_gain, pre_ffn_gain, post_ffn_gain,
           w_ffn_gate, w_ffn_up, w_ffn_down, ple_gain, w_ple_gate, w_ple_proj):
    batch, seq, d = x.shape
    n = batch * seq
    depth = w_in.shape[0]
    half = HEAD_DIM // 2

    lb_soft = jax.nn.softmax(hgrn_lb_logits.astype(F32), axis=0)
    lower_bounds = (jnp.cumsum(lb_soft, axis=0) - lb_soft[0:1]).reshape(depth, 1, HGRN_WIDTH)

    inv_freq = ROPE_THETA ** (-jnp.arange(half, dtype=F32) / half)
    inv = jnp.tile(inv_freq, LANES // half).reshape(1, LANES)
    sgn = jnp.tile(jnp.concatenate([-jnp.ones((half,), F32), jnp.ones((half,), F32)]),
                   LANES // HEAD_DIM).reshape(1, LANES)
    cos, sin = _rope_tables(positions.reshape(n, 1), inv, sgn)

    w_in_b = w_in.astype(BF16)
    w_out_b = w_out.astype(BF16)
    w_gate_b = w_ffn_gate.astype(BF16)
    w_up_b = w_ffn_up.astype(BF16)
    w_down_b = w_ffn_down.astype(BF16)
    w_pg_b = w_ple_gate.astype(BF16)
    w_pp_b = w_ple_proj.astype(BF16)
    p2 = p.reshape(depth, n, D_PLE)

    vecs = lambda a: a.astype(F32).reshape(depth, 1, -1)
    pre_mix, post_mix = vecs(pre_mix_gain), vecs(post_mix_gain)
    pre_ffn, post_ffn = vecs(pre_ffn_gain), vecs(post_ffn_gain)
    ple_g, attn_g, hgrn_g = vecs(ple_gain), vecs(attn_out_gain), vecs(hgrn_out_gain)
    sinks = attn_sinks.astype(F32)

    xf = x.reshape(n, d).astype(F32)
    h = _norm(xf, pre_mix, 0)
    for i in range(depth):
        xf, h = _mixer(sinks, h, xf, w_in_b, w_out_b, cos, sin, lower_bounds, attn_g, hgrn_g,
                       post_mix, pre_ffn, i, batch, seq)
        xf, h = _ffn(h, w_gate_b, w_up_b, w_down_b, xf, post_ffn, ple_g,
                     w_pg_b, p2, w_pp_b, pre_mix, i, last=(i + 1 == depth))
    return xf.reshape(batch, seq, d).astype(x.dtype)
```
